```python
import jax, jax.numpy as jnp
from jax import lax
import numpy as np

D_MODEL = 1024
BATCH = 32
SEQ = 2048
DEPTH = 2
DEC_BATCH = 32
DEC_SEQ = 64
PAST_LEN = 2048

CHUNK = 64
N_EVEN = (DEPTH + 1) // 2
N_ODD = DEPTH // 2
D_POOL = D_MODEL // 2
POOL_WINDOWS = (2, 4, 8, 16)
N_POOL_GROUPS = len(POOL_WINDOWS)
POOL_GROUP = D_POOL // N_POOL_GROUPS
POOL_STATE = max(POOL_WINDOWS) - 1
D_RNN = D_MODEL // 2
N_RNN_BLOCKS = 8
RNN_BLOCK = D_RNN // N_RNN_BLOCKS
CONV_WIDTH = 4
RG_C = 8.0
HEAD_DIM = 64
N_Q_HEADS = D_MODEL // HEAD_DIM
N_KV_HEADS = 4
GROUP = N_Q_HEADS // N_KV_HEADS
WINDOW = 128
N_MEM = 256
N_MEM_HEADS = 4
MEM_HEAD_DIM = D_MODEL // N_MEM_HEADS
D_FF = 4 * D_MODEL
EPS = 1e-6
NEG_INF = -1e30

kernel_name = 'hybrid_stream_pool_rglru_swa_step'


def rmsnorm(x, g):
    x32 = x.astype(jnp.float32)
    y = x32 * lax.rsqrt(jnp.mean(x32 * x32, axis=-1, keepdims=True) + EPS) * g.astype(jnp.float32)
    return y.astype(x.dtype)


def pool_mixer(u, prev, pos0, w, scale):
    B, L, _ = u.shape
    up = jnp.concatenate([prev, u], axis=1)
    cs = jnp.cumsum(up.astype(jnp.float32), axis=1)
    cs = jnp.pad(cs, ((0, 0), (1, 0), (0, 0)))
    pos = (pos0 + jnp.arange(L, dtype=jnp.int32))[None, :, None]
    end = cs[:, POOL_STATE + 1:]
    means = []
    for g, win in enumerate(POOL_WINDOWS):
        sl = slice(g * POOL_GROUP, (g + 1) * POOL_GROUP)
        start = cs[:, POOL_STATE + 1 - win:POOL_STATE + 1 - win + L, sl]
        cnt = jnp.minimum(pos + 1, win).astype(jnp.float32)
        means.append((end[..., sl] - start) / cnt)
    d = (jnp.concatenate(means, axis=-1) - u.astype(jnp.float32)).astype(u.dtype)
    d = d.reshape(B, L, N_POOL_GROUPS, POOL_GROUP)
    y = jnp.einsum('blgc,gce->blge', d, w).reshape(B, L, D_POOL) * scale
    return y, up[:, -POOL_STATE:]


def causal_conv(u, prev, w, b):
    L = u.shape[1]
    up = jnp.concatenate([prev, u], axis=1)
    y = b + sum(up[:, k:k + L] * w[k] for k in range(CONV_WIDTH))
    return y, up[:, -(CONV_WIDTH - 1):]


def rg_lru(xc, h0, w_a, b_a, w_x, b_x, lam):
    B, L, _ = xc.shape
    xb = xc.reshape(B, L, N_RNN_BLOCKS, RNN_BLOCK)
    r_gate = jax.nn.sigmoid(jnp.einsum('blhi,hij->blhj', xb, w_a).reshape(B, L, D_RNN).astype(jnp.float32)
                            + b_a.astype(jnp.float32))
    i_gate = jax.nn.sigmoid(jnp.einsum('blhi,hij->blhj', xb, w_x).reshape(B, L, D_RNN).astype(jnp.float32)
                            + b_x.astype(jnp.float32))
    log_a = -RG_C * r_gate * jax.nn.softplus(-lam.astype(jnp.float32))
    a = jnp.exp(log_a)
    mult = jnp.sqrt(-jnp.expm1(2.0 * log_a))
    b = mult * i_gate * xc.astype(jnp.float32)
    b = b.at[:, 0].add(a[:, 0] * h0.astype(jnp.float32))

    def combine(lhs, rhs):
        a1, b1 = lhs
        a2, b2 = rhs
        return a1 * a2, a2 * b1 + b2

    _, h = lax.associative_scan(combine, (a, b), axis=1)
    return h.astype(xc.dtype), h[:, -1].astype(xc.dtype)


def swa_attention(q, k_all, v_all, prefix_valid, sinks):
    B, L = q.shape[0], q.shape[1]
    n_blk = -(-L // CHUNK)
    pad = n_blk * CHUNK - L
    q = jnp.pad(q, ((0, 0), (0, pad), (0, 0), (0, 0), (0, 0)))
    k_all = jnp.pad(k_all, ((0, 0), (0, pad), (0, 0), (0, 0)))
    v_all = jnp.pad(v_all, ((0, 0), (0, pad), (0, 0), (0, 0)))
    valid = jnp.concatenate([jnp.full((WINDOW,), prefix_valid, dtype=bool),
                             jnp.arange(n_blk * CHUNK) < L])
    span = WINDOW + CHUNK
    scale = HEAD_DIM ** -0.5
    sink = sinks.astype(jnp.float32).reshape(N_KV_HEADS, GROUP, 1, 1)

    def block(n):
        start = n * CHUNK
        qs = lax.dynamic_slice_in_dim(q, start, CHUNK, axis=1)
        kb = lax.dynamic_slice_in_dim(k_all, start, span, axis=1)
        vb = lax.dynamic_slice_in_dim(v_all, start, span, axis=1)
        mb = lax.dynamic_slice_in_dim(valid, start, span, axis=0)
        s = jnp.einsum('bqkgd,bmkd->bkgqm', qs, kb).astype(jnp.float32) * scale
        s = jnp.where(mb, s, NEG_INF)
        snk = jnp.broadcast_to(sink, s.shape[:-1] + (1,))
        pr = jax.nn.softmax(jnp.concatenate([s, snk], axis=-1), axis=-1)[..., :span]
        return jnp.einsum('bkgqm,bmkd->bqkgd', pr.astype(vb.dtype), vb)

    o = lax.map(block, jnp.arange(n_blk))
    o = jnp.moveaxis(o, 0, 1).reshape(B, n_blk * CHUNK, N_Q_HEADS * HEAD_DIM)
    return o[:, :L]


def even_mixer(h, pool_prev, conv_prev, lru_prev, pos0, p, e):
    z = h @ p['w_in_even'][e]
    u_pool = z[..., :D_POOL]
    u_x = z[..., D_POOL:D_POOL + D_RNN]
    u_gate = z[..., D_POOL + D_RNN:]
    y_pool, pool_new = pool_mixer(u_pool, pool_prev, pos0, p['pool_w'][e], p['pool_scale'][e])
    xc, conv_new = causal_conv(u_x, conv_prev, p['conv_w'][e], p['conv_b'][e])
    hr, lru_new = rg_lru(xc, lru_prev, p['w_rg_a'][e], p['b_rg_a'][e], p['w_rg_x'][e], p['b_rg_x'][e],
                         p['rg_lambda'][e])
    y_rnn = hr * jax.nn.gelu(u_gate)
    y = jnp.concatenate([y_pool, y_rnn], axis=-1) @ p['w_out_even'][e]
    return y, pool_new, conv_new, lru_new


def odd_mixer(h, k_prev, v_prev, prefix_valid, p, o):
    B, L, _ = h.shape
    z = h @ p['w_qkv_odd'][o]
    nq = N_Q_HEADS * HEAD_DIM
    nkv = N_KV_HEADS * HEAD_DIM
    q = z[..., :nq].reshape(B, L, N_KV_HEADS, GROUP, HEAD_DIM)
    k = z[..., nq:nq + nkv].reshape(B, L, N_KV_HEADS, HEAD_DIM)
    v = z[..., nq + nkv:].reshape(B, L, N_KV_HEADS, HEAD_DIM)
    k_all = jnp.concatenate([k_prev, k], axis=1)
    v_all = jnp.concatenate([v_prev, v], axis=1)
    att = swa_attention(q, k_all, v_all, prefix_valid, p['attn_sinks'][o])
    y = att @ p['w_o_odd'][o]
    return y, k_all[:, -WINDOW:], v_all[:, -WINDOW:]


def mem_kv(mem, g, w_k, w_v):
    B = mem.shape[0]
    mn = rmsnorm(mem, g)
    k = (mn @ w_k).reshape(B, N_MEM, N_MEM_HEADS, MEM_HEAD_DIM)
    v = (mn @ w_v).reshape(B, N_MEM, N_MEM_HEADS, MEM_HEAD_DIM)
    return k, v


def cross_attn(h, mk, mv, w_q, w_o):
    B, L, _ = h.shape
    q = (h @ w_q).reshape(B, L, N_MEM_HEADS, MEM_HEAD_DIM)
    s = jnp.einsum('blhd,bmhd->bhlm', q, mk).astype(jnp.float32) * (MEM_HEAD_DIM ** -0.5)
    pr = jax.nn.softmax(s, axis=-1)
    o = jnp.einsum('bhlm,bmhd->blhd', pr.astype(mv.dtype), mv).reshape(B, L, D_MODEL)
    return o @ w_o


def sq_relu_mlp(h, w_up, w_down):
    return jnp.square(jax.nn.relu(h @ w_up)) @ w_down


def trunk(x, pos0, prefix_valid, pool_st, conv_st, lru_st, swa_k, swa_v, mem_k, mem_v, p):
    new_pool, new_conv, new_lru, new_k, new_v = [], [], [], [], []
    for layer in range(DEPTH):
        h = rmsnorm(x, p['norm_mix'][layer])
        if layer % 2 == 0:
            e = layer // 2
            y, pn, cn, ln = even_mixer(h, pool_st[e], conv_st[e], lru_st[e], pos0, p, e)
            new_pool.append(pn)
            new_conv.append(cn)
            new_lru.append(ln)
        else:
            o = layer // 2
            y, kn, vn = odd_mixer(h, swa_k[o], swa_v[o], prefix_valid, p, o)
            new_k.append(kn)
            new_v.append(vn)
        x = x + y
        x = x + cross_attn(rmsnorm(x, p['norm_cross'][layer]), mem_k[layer], mem_v[layer],
                           p['w_mq'][layer], p['w_mo'][layer])
        x = x + sq_relu_mlp(rmsnorm(x, p['norm_mlp'][layer]), p['w_up'][layer], p['w_down'][layer])
    y = rmsnorm(x, p['norm_final'])
    return (y, jnp.stack(new_pool), jnp.stack(new_conv), jnp.stack(new_lru),
            jnp.stack(new_k), jnp.stack(new_v))


def setup_inputs(seed: int = 0) -> dict:
    key = jax.random.key(seed)
    ks = iter(jax.random.split(key, 64))

    def nrm(shape, scale):
        return jax.random.normal(next(ks), shape, jnp.float32) * scale

    def gain(shape):
        return 1.0 + 0.05 * jax.random.normal(next(ks), shape, jnp.float32)

    a_c = jax.random.uniform(next(ks), (N_EVEN, D_RNN), jnp.float32, 0.9, 0.999)
    s = a_c ** (1.0 / RG_C)
    rg_lambda = jnp.log(s) - jnp.log1p(-s)
    d_in_even = D_POOL + 2 * D_RNN
    d_qkv = (N_Q_HEADS + 2 * N_KV_HEADS) * HEAD_DIM
    return dict(
        x_prompt=nrm((BATCH, SEQ, D_MODEL), 1.0),
        x_sample=nrm((DEC_BATCH, DEC_SEQ, D_MODEL), 1.0),
        state_pool=nrm((N_EVEN, DEC_BATCH, POOL_STATE, D_POOL), 1.0),
        state_conv=nrm((N_EVEN, DEC_BATCH, CONV_WIDTH - 1, D_RNN), 1.0),
        state_lru=nrm((N_EVEN, DEC_BATCH, D_RNN), 0.5),
        cache_swa_k=nrm((N_ODD, DEC_BATCH, WINDOW, N_KV_HEADS, HEAD_DIM), 1.0),
        cache_swa_v=nrm((N_ODD, DEC_BATCH, WINDOW, N_KV_HEADS, HEAD_DIM), 1.0),
        cache_mem_k=nrm((DEPTH, DEC_BATCH, N_MEM, N_MEM_HEADS, MEM_HEAD_DIM), 1.0),
        cache_mem_v=nrm((DEPTH, DEC_BATCH, N_MEM, N_MEM_HEADS, MEM_HEAD_DIM), 1.0),
        mem_prompt=nrm((BATCH, N_MEM, D_MODEL), 1.0),
        norm_mix=gain((DEPTH, D_MODEL)),
        norm_cross=gain((DEPTH, D_MODEL)),
        norm_mem=gain((DEPTH, D_MODEL)),
        norm_mlp=gain((DEPTH, D_MODEL)),
        norm_final=gain((D_MODEL,)),
        w_in_even=nrm((N_EVEN, D_MODEL, d_in_even), D_MODEL ** -0.5),
        conv_w=nrm((N_EVEN, CONV_WIDTH, D_RNN), CONV_WIDTH ** -0.5),
        conv_b=nrm((N_EVEN, D_RNN), 0.01),
        w_rg_a=nrm((N_EVEN, N_RNN_BLOCKS, RNN_BLOCK, RNN_BLOCK), RNN_BLOCK ** -0.5),
        b_rg_a=nrm((N_EVEN, D_RNN), 0.01),
        w_rg_x=nrm((N_EVEN, N_RNN_BLOCKS, RNN_BLOCK, RNN_BLOCK), RNN_BLOCK ** -0.5),
        b_rg_x=nrm((N_EVEN, D_RNN), 0.01),
        rg_lambda=rg_lambda,
        pool_w=nrm((N_EVEN, N_POOL_GROUPS, POOL_GROUP, POOL_GROUP), POOL_GROUP ** -0.5),
        pool_scale=gain((N_EVEN, D_POOL)),
        w_out_even=nrm((N_EVEN, D_POOL + D_RNN, D_MODEL), (D_POOL + D_RNN) ** -0.5),
        w_qkv_odd=nrm((N_ODD, D_MODEL, d_qkv), D_MODEL ** -0.5),
        attn_sinks=nrm((N_ODD, N_Q_HEADS), 0.5),
        w_o_odd=nrm((N_ODD, N_Q_HEADS * HEAD_DIM, D_MODEL), (N_Q_HEADS * HEAD_DIM) ** -0.5),
        w_mq=nrm((DEPTH, D_MODEL, D_MODEL), D_MODEL ** -0.5),
        w_mk=nrm((DEPTH, D_MODEL, D_MODEL), D_MODEL ** -0.5),
        w_mv=nrm((DEPTH, D_MODEL, D_MODEL), D_MODEL ** -0.5),
        w_mo=nrm((DEPTH, D_MODEL, D_MODEL), D_MODEL ** -0.5),
        w_up=nrm((DEPTH, D_MODEL, D_FF), D_MODEL ** -0.5),
        w_down=nrm((DEPTH, D_FF, D_MODEL), D_FF ** -0.5),
    )


def reference(x_prompt, x_sample, state_pool, state_conv, state_lru, cache_swa_k, cache_swa_v,
              cache_mem_k, cache_mem_v, mem_prompt, norm_mix, norm_cross, norm_mem, norm_mlp, norm_final,
              w_in_even, conv_w, conv_b, w_rg_a, b_rg_a, w_rg_x, b_rg_x, rg_lambda, pool_w, pool_scale,
              w_out_even, w_qkv_odd, attn_sinks, w_o_odd, w_mq, w_mk, w_mv, w_mo, w_up, w_down):
    p = dict(norm_mix=norm_mix, norm_cross=norm_cross, norm_mlp=norm_mlp, norm_final=norm_final,
             w_in_even=w_in_even, conv_w=conv_w, conv_b=conv_b, w_rg_a=w_rg_a, b_rg_a=b_rg_a,
             w_rg_x=w_rg_x, b_rg_x=b_rg_x, rg_lambda=rg_lambda, pool_w=pool_w, pool_scale=pool_scale,
             w_out_even=w_out_even, w_qkv_odd=w_qkv_odd, attn_sinks=attn_sinks, w_o_odd=w_o_odd,
             w_mq=w_mq, w_mo=w_mo, w_up=w_up, w_down=w_down)
    B = x_prompt.shape[0]
    dt = x_prompt.dtype
    mks, mvs = [], []
    for layer in range(DEPTH):
        mk, mv = mem_kv(mem_prompt, norm_mem[layer], w_mk[layer], w_mv[layer])
        mks.append(mk)
        mvs.append(mv)
    mem_k_p = jnp.stack(mks)
    mem_v_p = jnp.stack(mvs)
    zero_pool = jnp.zeros((N_EVEN, B, POOL_STATE, D_POOL), dt)
    zero_conv = jnp.zeros((N_EVEN, B, CONV_WIDTH - 1, D_RNN), dt)
    zero_lru = jnp.zeros((N_EVEN, B, D_RNN), dt)
    zero_kv = jnp.zeros((N_ODD, B, WINDOW, N_KV_HEADS, HEAD_DIM), dt)
    y_prompt, pool_p, conv_p, lru_p, swa_k_p, swa_v_p = trunk(
        x_prompt, 0, False, zero_pool, zero_conv, zero_lru, zero_kv, zero_kv, mem_k_p, mem_v_p, p)
    y_sample, pool_s, conv_s, lru_s, swa_k_s, swa_v_s = trunk(
        x_sample, PAST_LEN, True, state_pool, state_conv, state_lru, cache_swa_k, cache_swa_v,
        cache_mem_k, cache_mem_v, p)
    return (y_prompt, y_sample, pool_p, conv_p, lru_p, swa_k_p, swa_v_p, mem_k_p, mem_v_p,
            pool_s, conv_s, lru_s, swa_k_s, swa_v_s)
```

```python
import functools

import jax
import jax.numpy as jnp
from jax import lax
from jax.experimental import pallas as pl
from jax.experimental.pallas import tpu as pltpu

F32 = jnp.float32
BF16 = jnp.bfloat16

EPS = 1e-6
NEG_INF = -1e30
CHUNK = 64
POOL_WINDOWS = (2, 4, 8, 16)
N_RNN_BLOCKS = 8
CONV_WIDTH = 4
RG_C = 8.0
HEAD_DIM = 64
N_KV_HEADS = 4
WINDOW = 128
N_MEM_HEADS = 4

LANES = 128
SUBLANES = 8
MXU_DIM = 256
VMEM_LIMIT_BYTES = 56 * 1024 * 1024

POOL_HIST = 16
CONV_HIST = 8
KEY_SPAN = WINDOW + 2 * CHUNK
KV_HIST = KEY_SPAN - CHUNK
CROSS_MAX_BATCH_BLOCK = 4
PAST_LEN = 2048


def _cparams():
    return pltpu.CompilerParams(
        dimension_semantics=("arbitrary", "arbitrary"),
        vmem_limit_bytes=VMEM_LIMIT_BYTES)


def _const_spec(shape):
    zeros = (0,) * len(shape)
    return pl.BlockSpec(shape, lambda *_: zeros, pipeline_mode=pl.Buffered(1))


def _rms(x, g):
    ms = jnp.mean(x * x, axis=-1, keepdims=True)
    return x * lax.rsqrt(ms + EPS) * g


def _scan_pitch(tile):
    p = -(-tile // SUBLANES)
    while p % 8 != 4:
        p += 1
    return p


def _mlp_kernel(x_ref, g_ref, wu_ref, wd_ref, *rest, final_norm, ff_chunk):
    if final_norm:
        gf_ref, o_ref, a_scr = rest
    else:
        o_ref, a_scr = rest
    x = x_ref[0]
    h = _rms(x, g_ref[...]).astype(BF16)
    d_ff = wu_ref.shape[1]
    for c in range(d_ff // ff_chunk):
        cols = slice(c * ff_chunk, (c + 1) * ff_chunk)
        u = jnp.dot(h, wu_ref[:, cols], preferred_element_type=F32)
        r = jnp.maximum(u, 0.0)
        a_scr[:, cols] = (r * r).astype(BF16)
    y = x + jnp.dot(a_scr[...], wd_ref[...], preferred_element_type=F32)
    if final_norm:
        y = _rms(y, gf_ref[...])
    o_ref[0] = y


def _mlp(x, g, w_up, w_down, g_final, tile):
    B, L, D = x.shape
    d_ff = w_up.shape[1]
    final_norm = g_final is not None
    in_specs = [
        pl.BlockSpec((1, tile, D), lambda b, j: (b, j, 0)),
        _const_spec((1, D)),
        _const_spec((D, d_ff)),
        _const_spec((d_ff, D)),
    ]
    args = [x, g.reshape(1, D), w_up, w_down]
    if final_norm:
        in_specs.append(_const_spec((1, D)))
        args.append(g_final.reshape(1, D))
    return pl.pallas_call(
        functools.partial(_mlp_kernel, final_norm=final_norm, ff_chunk=4 * MXU_DIM),
        grid=(B, L // tile),
        in_specs=in_specs,
        out_specs=pl.BlockSpec((1, tile, D), lambda b, j: (b, j, 0)),
        out_shape=jax.ShapeDtypeStruct((B, L, D), F32),
        scratch_shapes=[pltpu.VMEM((tile, d_ff), BF16)],
        compiler_params=_cparams(),
        name="mlp",
    )(*args)


def _cross_kernel(x_ref, g_ref, wq_ref, wo_ref, mk_ref, mv_ref, o_ref, k_scr, v_scr, att_scr):
    bt, tile, D = x_ref.shape

    @pl.when(pl.program_id(1) == 0)
    def _():
        k_scr[...] = mk_ref[...].astype(BF16)
        v_scr[...] = mv_ref[...].astype(BF16)

    dh = D // N_MEM_HEADS
    x = x_ref[...].reshape(bt * tile, D)
    h = _rms(x, g_ref[...]).astype(BF16)
    q = jnp.dot(h, wq_ref[...], preferred_element_type=F32)
    q = (q * (dh ** -0.5)).astype(BF16)
    for bi in range(bt):
        rows = slice(bi * tile, (bi + 1) * tile)
        for hd in range(N_MEM_HEADS):
            cols = slice(hd * dh, (hd + 1) * dh)
            s = lax.dot_general(q[rows, cols], k_scr[bi, :, cols], (((1,), (1,)), ((), ())),
                                preferred_element_type=F32)
            mx = jnp.max(s, axis=-1, keepdims=True)
            p = jnp.exp(s - mx)
            den = jnp.sum(p, axis=-1, keepdims=True)
            o = jnp.dot(p.astype(BF16), v_scr[bi, :, cols], preferred_element_type=F32)
            att_scr[rows, cols] = (o * (1.0 / den)).astype(BF16)
    y = x + jnp.dot(att_scr[...], wo_ref[...], preferred_element_type=F32)
    o_ref[...] = y.reshape(bt, tile, D)


def _cross(x, g, w_q, w_o, mem_k, mem_v, tile, bt):
    B, L, D = x.shape
    n_mem = mem_k.shape[1]
    return pl.pallas_call(
        _cross_kernel,
        grid=(B // bt, L // tile),
        in_specs=[
            pl.BlockSpec((bt, tile, D), lambda b, j: (b, j, 0)),
            _const_spec((1, D)),
            _const_spec((D, D)),
            _const_spec((D, D)),
            pl.BlockSpec((bt, n_mem, D), lambda b, j: (b, 0, 0)),
            pl.BlockSpec((bt, n_mem, D), lambda b, j: (b, 0, 0)),
        ],
        out_specs=pl.BlockSpec((bt, tile, D), lambda b, j: (b, j, 0)),
        out_shape=jax.ShapeDtypeStruct((B, L, D), F32),
        scratch_shapes=[pltpu.VMEM((bt, n_mem, D), BF16), pltpu.VMEM((bt, n_mem, D), BF16),
                        pltpu.VMEM((bt * tile, D), BF16)],
        compiler_params=_cparams(),
        name="cross_attn",
    )(x, g.reshape(1, D), w_q, w_o, mem_k, mem_v)


def _memkv_kernel(m_ref, g_ref, wk_ref, wv_ref, k_ref, v_ref):
    mn = _rms(m_ref[...], g_ref[0]).astype(BF16)
    k_ref[0] = jnp.dot(mn, wk_ref[0], preferred_element_type=F32)
    v_ref[0] = jnp.dot(mn, wv_ref[0], preferred_element_type=F32)


def _mem_kv(mem, g, w_k, w_v, tile):
    depth, D = g.shape
    T = mem.shape[0]
    out = jax.ShapeDtypeStruct((depth, T, D), F32)
    return pl.pallas_call(
        _memkv_kernel,
        grid=(depth, T // tile),
        in_specs=[
            pl.BlockSpec((tile, D), lambda l, j: (j, 0)),
            pl.BlockSpec((1, 1, D), lambda l, j: (l, 0, 0)),
            pl.BlockSpec((1, D, D), lambda l, j: (l, 0, 0)),
            pl.BlockSpec((1, D, D), lambda l, j: (l, 0, 0)),
        ],
        out_specs=[pl.BlockSpec((1, tile, D), lambda l, j: (l, j, 0)),
                   pl.BlockSpec((1, tile, D), lambda l, j: (l, j, 0))],
        out_shape=[out, out],
        compiler_params=_cparams(),
        name="mem_kv",
    )(mem, g.reshape(depth, 1, D), w_k, w_v)


def _even_kernel(x_ref, g_ref, win_ref, pbd_ref, pscale_ref, cw_ref, cb_ref, wax_ref, ba_ref, bx_ref,
                 lam_ref, wout_ref, pool0_ref, conv0_ref, lru0_ref,
                 o_ref, pool_o, conv_o, lru_o,
                 ubuf, xbuf, a_s, b_s, h_s, p_s, hcar, cat_scr, *, pos0, pitch):
    j = pl.program_id(1)
    tl = x_ref.shape[1]
    dp = pbd_ref.shape[0] * pbd_ref.shape[1]
    dr = cw_ref.shape[1]
    n_slab = dr // LANES

    @pl.when(j == 0)
    def _():
        ubuf[0:POOL_HIST, :] = pool0_ref[0]
        xbuf[0:CONV_HIST, :] = conv0_ref[0]
        hcar[...] = lru0_ref[0]
        a_s[:, tl:, :] = jnp.zeros((n_slab, SUBLANES * pitch - tl, LANES), F32)
        b_s[:, tl:, :] = jnp.zeros((n_slab, SUBLANES * pitch - tl, LANES), F32)

    x = x_ref[0]
    h = _rms(x, g_ref[...]).astype(BF16)
    z = jnp.dot(h, win_ref[...], preferred_element_type=F32)
    u_pool = z[:, :dp]
    u_x = z[:, dp:dp + dr]
    u_gate = z[:, dp + dr:]
    ubuf[POOL_HIST:POOL_HIST + tl, :] = u_pool
    xbuf[CONV_HIST:CONV_HIST + tl, :] = u_x

    pos = pos0 + j * tl + lax.broadcasted_iota(jnp.int32, (tl, 1), 0)
    group = dp // len(POOL_WINDOWS)
    for gi, win in enumerate(POOL_WINDOWS):
        cols = slice(gi * group, (gi + 1) * group)
        s = u_pool[:, cols]
        for k in range(1, win):
            s = s + ubuf[POOL_HIST - k:POOL_HIST - k + tl, cols]
        inv_cnt = 1.0 / jnp.minimum(pos + 1, win).astype(F32)
        cat_scr[:, cols] = (s * inv_cnt - u_pool[:, cols]).astype(BF16)
    half = MXU_DIM
    for hf in range(dp // half):
        cols = slice(hf * half, (hf + 1) * half)
        yp = jnp.dot(cat_scr[:, cols], pbd_ref[hf], preferred_element_type=F32)
        cat_scr[:, cols] = (yp * pscale_ref[:, cols]).astype(BF16)

    xc = cb_ref[...] + u_x * cw_ref[CONV_WIDTH - 1:CONV_WIDTH, :]
    for k in range(CONV_WIDTH - 1):
        sh = CONV_WIDTH - 1 - k
        xc = xc + xbuf[CONV_HIST - sh:CONV_HIST - sh + tl, :] * cw_ref[k:k + 1, :]

    xcb = xc.astype(BF16)
    c_lam = -RG_C * jnp.log1p(jnp.exp(-lam_ref[...]))
    for hf in range(dr // half):
        cols = slice(hf * half, (hf + 1) * half)
        gates = jnp.dot(xcb[:, cols], wax_ref[hf], preferred_element_type=F32)
        r_gate = jax.nn.sigmoid(gates[:, :half] + ba_ref[:, cols])
        i_gate = jax.nn.sigmoid(gates[:, half:] + bx_ref[:, cols])
        log_a = r_gate * c_lam[:, cols]
        a = jnp.exp(log_a)
        mult = jnp.sqrt(-jnp.tanh(log_a) * (a * a + 1.0))
        bb = mult * i_gate * xc[:, cols]
        for sl in range(half // LANES):
            k = hf * (half // LANES) + sl
            a_s[k, 0:tl, :] = a[:, sl * LANES:(sl + 1) * LANES]
            b_s[k, 0:tl, :] = bb[:, sl * LANES:(sl + 1) * LANES]

    for k in range(n_slab):
        a_k, b_k, h_k, p_k = a_s.at[k], b_s.at[k], h_s.at[k], p_s.at[k]
        hv = jnp.zeros((SUBLANES, LANES), F32)
        pv = jnp.ones((SUBLANES, LANES), F32)
        for i in range(pitch):
            rows = pl.ds(i, SUBLANES, stride=pitch)
            av = a_k[rows, :]
            hv = av * hv + b_k[rows, :]
            pv = av * pv
            h_k[rows, :] = hv
            p_k[rows, :] = pv
        carry = [hcar[:, k * LANES:(k + 1) * LANES]]
        for r in range(1, SUBLANES):
            carry.append(hv[r - 1:r, :] + pv[r - 1:r, :] * carry[-1])
        cv = jnp.concatenate(carry, axis=0)
        for i in range(pitch):
            rows = pl.ds(i, SUBLANES, stride=pitch)
            h_k[rows, :] = h_k[rows, :] + p_k[rows, :] * cv
        hcar[:, k * LANES:(k + 1) * LANES] = h_k[tl - 1:tl, :]

    hr = jnp.concatenate([h_s[k, 0:tl, :] for k in range(n_slab)], axis=1)
    cat_scr[:, dp:] = (hr * jax.nn.gelu(u_gate)).astype(BF16)
    o_ref[0] = x + jnp.dot(cat_scr[...], wout_ref[...], preferred_element_type=F32)

    pool_o[0] = u_pool[tl - POOL_HIST:, :]
    conv_o[0] = u_x[tl - CONV_HIST:, :]
    lru_o[0] = hr[tl - SUBLANES:, :]
    ubuf[0:POOL_HIST, :] = u_pool[tl - POOL_HIST:, :]
    xbuf[0:CONV_HIST, :] = u_x[tl - CONV_HIST:, :]


def _even_layer(x, g, wts, pool0, conv0, lru0, pos0, tile):
    B, L, D = x.shape
    dp = pool0.shape[2]
    dr = conv0.shape[2]
    pitch = _scan_pitch(tile)
    n_slab = dr // LANES
    row = lambda v: v.reshape(1, -1)
    consts = [row(g), wts["w_in"], wts["pool_bd"], row(wts["pool_scale"]), wts["conv_w"],
              row(wts["conv_b"]), wts["w_ax"], row(wts["b_a"]), row(wts["b_x"]),
              row(wts["lam"]), wts["w_out"]]
    in_specs = [pl.BlockSpec((1, tile, D), lambda b, j: (b, j, 0))]
    in_specs += [_const_spec(c.shape) for c in consts]
    in_specs += [
        pl.BlockSpec((1, POOL_HIST, dp), lambda b, j: (b, 0, 0)),
        pl.BlockSpec((1, CONV_HIST, dr), lambda b, j: (b, 0, 0)),
        pl.BlockSpec((1, 1, dr), lambda b, j: (b, 0, 0)),
    ]
    slab = pltpu.VMEM((n_slab, SUBLANES * pitch, LANES), F32)
    return pl.pallas_call(
        functools.partial(_even_kernel, pos0=pos0, pitch=pitch),
        grid=(B, L // tile),
        in_specs=in_specs,
        out_specs=[
            pl.BlockSpec((1, tile, D), lambda b, j: (b, j, 0)),
            pl.BlockSpec((1, POOL_HIST, dp), lambda b, j: (b, 0, 0)),
            pl.BlockSpec((1, CONV_HIST, dr), lambda b, j: (b, 0, 0)),
            pl.BlockSpec((1, SUBLANES, dr), lambda b, j: (b, 0, 0)),
        ],
        out_shape=[
            jax.ShapeDtypeStruct((B, L, D), F32),
            jax.ShapeDtypeStruct((B, POOL_HIST, dp), F32),
            jax.ShapeDtypeStruct((B, CONV_HIST, dr), F32),
            jax.ShapeDtypeStruct((B, SUBLANES, dr), F32),
        ],
        scratch_shapes=[
            pltpu.VMEM((POOL_HIST + tile, dp), F32),
            pltpu.VMEM((CONV_HIST + tile, dr), F32),
            slab, slab, slab, slab,
            pltpu.VMEM((1, dr), F32),
            pltpu.VMEM((tile, dp + dr), BF16),
        ],
        compiler_params=_cparams(),
        name="even_mixer",
    )(x, *consts, pool0, conv0, lru0)


def _odd_kernel(x_ref, g_ref, wqkv_ref, wo_ref, sink_ref, kprev_ref, vprev_ref,
                o_ref, ktail_ref, vtail_ref,
                q_scr, k_buf, v_buf, att_scr, *, prefix_valid):
    j = pl.program_id(1)
    tq = x_ref.shape[1]
    nq = q_scr.shape[1]
    nkv = k_buf.shape[1]
    group = nq // nkv
    n_chunks = tq // CHUNK

    @pl.when(j == 0)
    def _():
        pad = jnp.zeros((KV_HIST - WINDOW, nkv), BF16)
        k_buf[0:KV_HIST - WINDOW, :] = pad
        v_buf[0:KV_HIST - WINDOW, :] = pad
        k_buf[KV_HIST - WINDOW:KV_HIST, :] = kprev_ref[0].astype(BF16)
        v_buf[KV_HIST - WINDOW:KV_HIST, :] = vprev_ref[0].astype(BF16)

    @pl.when(j > 0)
    def _():
        k_buf[0:KV_HIST, :] = k_buf[tq:tq + KV_HIST, :]
        v_buf[0:KV_HIST, :] = v_buf[tq:tq + KV_HIST, :]

    x = x_ref[0]
    h = _rms(x, g_ref[...]).astype(BF16)
    z = jnp.dot(h, wqkv_ref[...], preferred_element_type=F32)
    q_scr[...] = (z[:, :nq] * (HEAD_DIM ** -0.5)).astype(BF16)
    k_new = z[:, nq:nq + nkv]
    v_new = z[:, nq + nkv:]
    k_buf[KV_HIST:KV_HIST + tq, :] = k_new.astype(BF16)
    v_buf[KV_HIST:KV_HIST + tq, :] = v_new.astype(BF16)
    if tq >= WINDOW:
        ktail_ref[0] = k_new[tq - WINDOW:, :]
        vtail_ref[0] = v_new[tq - WINDOW:, :]
    else:
        ktail_ref[0] = jnp.concatenate([kprev_ref[0][tq:, :], k_new], axis=0)
        vtail_ref[0] = jnp.concatenate([vprev_ref[0][tq:, :], v_new], axis=0)

    lane = lax.broadcasted_iota(jnp.int32, (1, nkv), 1)
    head_masks = [(lane // HEAD_DIM) == kv for kv in range(N_KV_HEADS)]
    key_idx = lax.broadcasted_iota(jnp.int32, (1, KEY_SPAN), 1)

    def chunk_body(c, carry):
        r0 = pl.multiple_of(c * CHUNK, CHUNK)
        qs = jnp.concatenate([q_scr[pl.ds(r0, CHUNK), gi * nkv:(gi + 1) * nkv] for gi in range(group)],
                             axis=0)
        kspan = k_buf[pl.ds(r0, KEY_SPAN), :]
        vspan = v_buf[pl.ds(r0, KEY_SPAN), :]
        zero = jnp.zeros_like(kspan)
        kbd = jnp.concatenate([jnp.where(m, kspan, zero) for m in head_masks], axis=0)
        vbd = jnp.concatenate([jnp.where(m, vspan, zero) for m in head_masks], axis=0)
        s_all = lax.dot_general(qs, kbd, (((1,), (1,)), ((), ())), preferred_element_type=F32)
        key_pos = (j * tq - KV_HIST) + c * CHUNK + key_idx
        valid = key_idx >= (KEY_SPAN - WINDOW - CHUNK)
        if not prefix_valid:
            valid = valid & (key_pos >= 0)
        ps, invs = [], []
        for kv in range(N_KV_HEADS):
            s = jnp.where(valid, s_all[:, kv * KEY_SPAN:(kv + 1) * KEY_SPAN], NEG_INF)
            snk = sink_ref[:, kv:kv + 1]
            mx = jnp.maximum(jnp.max(s, axis=-1, keepdims=True), snk)
            p = jnp.exp(s - mx)
            den = jnp.sum(p, axis=-1, keepdims=True) + jnp.exp(snk - mx)
            ps.append(p.astype(BF16))
            invs.append(1.0 / den)
        o = jnp.dot(jnp.concatenate(ps, axis=1), vbd, preferred_element_type=F32)
        inv = invs[N_KV_HEADS - 1]
        for kv in range(N_KV_HEADS - 2, -1, -1):
            inv = jnp.where(head_masks[kv], invs[kv], inv)
        o = (o * inv).astype(BF16)
        for gi in range(group):
            att_scr[pl.ds(r0, CHUNK), gi * nkv:(gi + 1) * nkv] = o[gi * CHUNK:(gi + 1) * CHUNK, :]
        return carry

    lax.fori_loop(0, n_chunks, chunk_body, 0)
    o_ref[0] = x + jnp.dot(att_scr[...], wo_ref[...], preferred_element_type=F32)


def _odd_layer(x, g, w_qkv, w_o, sink_tab, k_prev, v_prev, prefix_valid, tile):
    B, L, D = x.shape
    nkv = k_prev.shape[2]
    nq = w_o.shape[0]
    tail = jax.ShapeDtypeStruct((B, WINDOW, nkv), F32)
    return pl.pallas_call(
        functools.partial(_odd_kernel, prefix_valid=prefix_valid),
        grid=(B, L // tile),
        in_specs=[
            pl.BlockSpec((1, tile, D), lambda b, j: (b, j, 0)),
            _const_spec((1, D)),
            _const_spec(w_qkv.shape),
            _const_spec(w_o.shape),
            _const_spec(sink_tab.shape),
            pl.BlockSpec((1, WINDOW, nkv), lambda b, j: (b, 0, 0)),
            pl.BlockSpec((1, WINDOW, nkv), lambda b, j: (b, 0, 0)),
        ],
        out_specs=[
            pl.BlockSpec((1, tile, D), lambda b, j: (b, j, 0)),
            pl.BlockSpec((1, WINDOW, nkv), lambda b, j: (b, 0, 0)),
            pl.BlockSpec((1, WINDOW, nkv), lambda b, j: (b, 0, 0)),
        ],
        out_shape=[jax.ShapeDtypeStruct((B, L, D), F32), tail, tail],
        scratch_shapes=[
            pltpu.VMEM((tile, nq), BF16),
            pltpu.VMEM((KV_HIST + tile, nkv), BF16),
            pltpu.VMEM((KV_HIST + tile, nkv), BF16),
            pltpu.VMEM((tile, nq), BF16),
        ],
        compiler_params=_cparams(),
        name="swa_mixer",
    )(x, g.reshape(1, D), w_qkv, w_o, sink_tab, k_prev, v_prev)


def _block_diag(blocks):
    n, r, c = blocks.shape
    eye = jnp.eye(n, dtype=blocks.dtype)
    return (eye[:, None, :, None] * blocks[:, :, None, :]).reshape(n * r, n * c)


def _prep_even(p, e):
    pool_w = p["pool_w"][e]
    per_tile = MXU_DIM // pool_w.shape[1]
    pool_bd = jnp.stack([_block_diag(pool_w[i:i + per_tile])
                         for i in range(0, pool_w.shape[0], per_tile)])
    wa, wx = p["w_rg_a"][e], p["w_rg_x"][e]
    per_tile = MXU_DIM // wa.shape[1]
    w_ax = jnp.stack([jnp.concatenate([_block_diag(wa[i:i + per_tile]), _block_diag(wx[i:i + per_tile])],
                                      axis=1)
                      for i in range(0, wa.shape[0], per_tile)])
    return dict(
        w_in=p["w_in_even"][e].astype(BF16), pool_bd=pool_bd.astype(BF16), pool_scale=p["pool_scale"][e],
        conv_w=p["conv_w"][e], conv_b=p["conv_b"][e], w_ax=w_ax.astype(BF16), b_a=p["b_rg_a"][e],
        b_x=p["b_rg_x"][e], lam=p["rg_lambda"][e], w_out=p["w_out_even"][e].astype(BF16))


def _prep_odd(p, o):
    w_qkv = p["w_qkv_odd"][o]
    D = w_qkv.shape[0]
    w_o = p["w_o_odd"][o]
    nq = w_o.shape[0]
    group = nq // (N_KV_HEADS * HEAD_DIM)
    wq = w_qkv[:, :nq].reshape(D, N_KV_HEADS, group, HEAD_DIM).transpose(0, 2, 1, 3).reshape(D, nq)
    w_qkv = jnp.concatenate([wq, w_qkv[:, nq:]], axis=1)
    w_o = w_o.reshape(N_KV_HEADS, group, HEAD_DIM, -1).transpose(1, 0, 2, 3).reshape(nq, -1)
    sinks = p["attn_sinks"][o].reshape(N_KV_HEADS, group).T
    sink_tab = jnp.repeat(sinks, CHUNK, axis=0).astype(F32)
    return dict(w_qkv=w_qkv.astype(BF16), w_o=w_o.astype(BF16), sink_tab=sink_tab)


def _pad_rows(a, rows):
    return jnp.pad(a, ((0, 0), (rows - a.shape[1], 0), (0, 0)))


def _trunk(x, pos0, prefix_valid, pool_st, conv_st, lru_st, swa_k, swa_v, mem_k, mem_v, p, prep,
           tile, tile_mix):
    depth = p["norm_mix"].shape[0]
    B = x.shape[0]
    new_pool, new_conv, new_lru, new_k, new_v = [], [], [], [], []
    for layer in range(depth):
        if layer % 2 == 0:
            e = layer // 2
            x, pn, cn, ln = _even_layer(
                x, p["norm_mix"][layer], prep["even"][e],
                _pad_rows(pool_st[e], POOL_HIST), _pad_rows(conv_st[e], CONV_HIST),
                lru_st[e][:, None, :], pos0, tile_mix)
            new_pool.append(pn[:, POOL_HIST - pool_st.shape[2]:])
            new_conv.append(cn[:, CONV_HIST - conv_st.shape[2]:])
            new_lru.append(ln[:, SUBLANES - 1])
        else:
            o = layer // 2
            kp = swa_k[o].reshape(B, WINDOW, -1)
            vp = swa_v[o].reshape(B, WINDOW, -1)
            x, kn, vn = _odd_layer(x, p["norm_mix"][layer], prep["odd"][o]["w_qkv"], prep["odd"][o]["w_o"],
                                   prep["odd"][o]["sink_tab"], kp, vp, prefix_valid, tile_mix)
            new_k.append(kn.reshape(swa_k[o].shape))
            new_v.append(vn.reshape(swa_v[o].shape))
        n_mem = mem_k.shape[2]
        L = x.shape[1]
        bt = max(1, tile // L)
        x = _cross(x, p["norm_cross"][layer], prep["w_mq"][layer], prep["w_mo"][layer],
                   mem_k[layer].reshape(B, n_mem, -1), mem_v[layer].reshape(B, n_mem, -1),
                   min(tile, L), min(bt, CROSS_MAX_BATCH_BLOCK, B))
        g_final = p["norm_final"] if layer == depth - 1 else None
        x = _mlp(x.reshape(1, B * L, -1), p["norm_mlp"][layer], prep["w_up"][layer], prep["w_down"][layer],
                 g_final, min(tile, B * L)).reshape(B, L, -1)
    return (x, jnp.stack(new_pool), jnp.stack(new_conv), jnp.stack(new_lru),
            jnp.stack(new_k), jnp.stack(new_v))


def kernel(x_prompt, x_sample, state_pool, state_conv, state_lru, cache_swa_k, cache_swa_v,
           cache_mem_k, cache_mem_v, mem_prompt, norm_mix, norm_cross, norm_mem, norm_mlp, norm_final,
           w_in_even, conv_w, conv_b, w_rg_a, b_rg_a, w_rg_x, b_rg_x, rg_lambda, pool_w, pool_scale,
           w_out_even, w_qkv_odd, attn_sinks, w_o_odd, w_mq, w_mk, w_mv, w_mo, w_up, w_down):
    p = dict(norm_mix=norm_mix, norm_cross=norm_cross, norm_mlp=norm_mlp, norm_final=norm_final,
             w_in_even=w_in_even, conv_w=conv_w, conv_b=conv_b, w_rg_a=w_rg_a, b_rg_a=b_rg_a,
             w_rg_x=w_rg_x, b_rg_x=b_rg_x, rg_lambda=rg_lambda, pool_w=pool_w, pool_scale=pool_scale,
             w_out_even=w_out_even, w_qkv_odd=w_qkv_odd, attn_sinks=attn_sinks, w_o_odd=w_o_odd)
    depth = norm_mix.shape[0]
    n_even, n_odd = state_pool.shape[0], cache_swa_k.shape[0]
    prep = dict(
        even=[_prep_even(p, e) for e in range(n_even)],
        odd=[_prep_odd(p, o) for o in range(n_odd)],
        w_mq=w_mq.astype(BF16), w_mo=w_mo.astype(BF16),
        w_up=w_up.astype(BF16), w_down=w_down.astype(BF16))

    B, L, D = x_prompt.shape
    dt = x_prompt.dtype
    n_mem = mem_prompt.shape[1]
    mem_shape = (depth, B, n_mem, N_MEM_HEADS, D // N_MEM_HEADS)
    mk, mv = _mem_kv(mem_prompt.reshape(B * n_mem, D), norm_mem, w_mk.astype(BF16), w_mv.astype(BF16),
                     tile=512)
    mem_k_p = mk.reshape(mem_shape)
    mem_v_p = mv.reshape(mem_shape)

    zero_pool = jnp.zeros((n_even, B) + state_pool.shape[2:], dt)
    zero_conv = jnp.zeros((n_even, B) + state_conv.shape[2:], dt)
    zero_lru = jnp.zeros((n_even, B) + state_lru.shape[2:], dt)
    zero_kv = jnp.zeros((n_odd, B) + cache_swa_k.shape[2:], dt)
    y_prompt, pool_p, conv_p, lru_p, swa_k_p, swa_v_p = _trunk(
        x_prompt, 0, False, zero_pool, zero_conv, zero_lru, zero_kv, zero_kv, mem_k_p, mem_v_p, p, prep,
        tile=512, tile_mix=256)
    Ls = x_sample.shape[1]
    y_sample, pool_s, conv_s, lru_s, swa_k_s, swa_v_s = _trunk(
        x_sample, PAST_LEN, True, state_pool, state_conv, state_lru, cache_swa_k, cache_swa_v,
        cache_mem_k, cache_mem_v, p, prep, tile=512, tile_mix=Ls)
    return (y_prompt, y_sample, pool_p, conv_p, lru_p, swa_k_p, swa_v_p, mem_k_p, mem_v_p,
            pool_s, conv_s, lru_s, swa_k_s, swa_v_s)
```

```python
import functools

import jax
import jax.numpy as jnp
from jax import lax
from jax.experimental import pallas as pl
from jax.experimental.pallas import tpu as pltpu

F32 = jnp.float32
BF16 = jnp.bfloat16

EPS = 1e-6
NEG_INF = -1e30
CHUNK = 64
POOL_WINDOWS = (2, 4, 8, 16)
N_RNN_BLOCKS = 8
CONV_WIDTH = 4
RG_C = 8.0
HEAD_DIM = 64
N_KV_HEADS = 4
WINDOW = 128
N_MEM_HEADS = 4

LANES = 128
SUBLANES = 8
MXU_DIM = 256
VMEM_LIMIT_BYTES = 56 * 1024 * 1024

POOL_HIST = 16
CONV_HIST = 8
KEY_SPAN = WINDOW + 2 * CHUNK
KV_HIST = KEY_SPAN - CHUNK
CROSS_MAX_BATCH_BLOCK = 4
PAST_LEN = 2048


def _cparams():
    return pltpu.CompilerParams(
        dimension_semantics=("arbitrary", "arbitrary"),
        vmem_limit_bytes=VMEM_LIMIT_BYTES)


def _const_spec(shape):
    zeros = (0,) * len(shape)
    return pl.BlockSpec(shape, lambda *_: zeros, pipeline_mode=pl.Buffered(1))


def _rms(x, g):
    ms = jnp.mean(x * x, axis=-1, keepdims=True)
    return x * lax.rsqrt(ms + EPS) * g


def _scan_pitch(tile):
    p = -(-tile // SUBLANES)
    while p % 8 != 4:
        p += 1
    return p


def _mlp_kernel(x_ref, g_ref, wu_ref, wd_ref, *rest, final_norm, ff_chunk):
    if final_norm:
        gf_ref, o_ref, a_scr = rest
    else:
        o_ref, a_scr = rest
    x = x_ref[0]
    h = _rms(x, g_ref[...]).astype(BF16)
    d_ff = wu_ref.shape[1]
    for c in range(d_ff // ff_chunk):
        cols = slice(c * ff_chunk, (c + 1) * ff_chunk)
        u = jnp.dot(h, wu_ref[:, cols], preferred_element_type=F32)
        r = jnp.maximum(u, 0.0)
        a_scr[:, cols] = (r * r).astype(BF16)
    y = x + jnp.dot(a_scr[...], wd_ref[...], preferred_element_type=F32)
    if final_norm:
        y = _rms(y, gf_ref[...])
    o_ref[0] = y


def _mlp(x, g, w_up, w_down, g_final, tile):
    B, L, D = x.shape
    d_ff = w_up.shape[1]
    final_norm = g_final is not None
    in_specs = [
        pl.BlockSpec((1, tile, D), lambda b, j: (b, j, 0)),
        _const_spec((1, D)),
        _const_spec((D, d_ff)),
        _const_spec((d_ff, D)),
    ]
    args = [x, g.reshape(1, D), w_up, w_down]
    if final_norm:
        in_specs.append(_const_spec((1, D)))
        args.append(g_final.reshape(1, D))
    return pl.pallas_call(
        functools.partial(_mlp_kernel, final_norm=final_norm, ff_chunk=4 * MXU_DIM),
        grid=(B, L // tile),
        in_specs=in_specs,
        out_specs=pl.BlockSpec((1, tile, D), lambda b, j: (b, j, 0)),
        out_shape=jax.ShapeDtypeStruct((B, L, D), F32),
        scratch_shapes=[pltpu.VMEM((tile, d_ff), BF16)],
        compiler_params=_cparams(),
        name="mlp",
    )(*args)


def _cross_kernel(x_ref, g_ref, wq_ref, wo_ref, mk_ref, mv_ref, o_ref, k_scr, v_scr, att_scr):
    bt, tile, D = x_ref.shape

    dh = D // N_MEM_HEADS

    @pl.when(pl.program_id(1) == 0)
    def _():
        for bi in range(bt):
            for hd in range(N_MEM_HEADS):
                cols = slice(hd * dh, (hd + 1) * dh)
                k_scr[bi, :, cols] = mk_ref[0, bi, :, hd, :].astype(BF16)
                v_scr[bi, :, cols] = mv_ref[0, bi, :, hd, :].astype(BF16)

    x = x_ref[...].reshape(bt * tile, D)
    h = _rms(x, g_ref[...]).astype(BF16)
    q = jnp.dot(h, wq_ref[...], preferred_element_type=F32)
    q = (q * (dh ** -0.5)).astype(BF16)
    for bi in range(bt):
        rows = slice(bi * tile, (bi + 1) * tile)
        for hd in range(N_MEM_HEADS):
            cols = slice(hd * dh, (hd + 1) * dh)
            s = lax.dot_general(q[rows, cols], k_scr[bi, :, cols], (((1,), (1,)), ((), ())),
                                preferred_element_type=F32)
            mx = jnp.max(s, axis=-1, keepdims=True)
            p = jnp.exp(s - mx)
            den = jnp.sum(p, axis=-1, keepdims=True)
            o = jnp.dot(p.astype(BF16), v_scr[bi, :, cols], preferred_element_type=F32)
            att_scr[rows, cols] = (o * (1.0 / den)).astype(BF16)
    y = x + jnp.dot(att_scr[...], wo_ref[...], preferred_element_type=F32)
    o_ref[...] = y.reshape(bt, tile, D)


def _cross(x, g, w_q, w_o, mem_k, mem_v, layer, tile, bt):
    B, L, D = x.shape
    n_mem = mem_k.shape[2]
    mem_spec = pl.BlockSpec((1, bt) + mem_k.shape[2:], lambda b, j: (layer, b, 0, 0, 0))
    return pl.pallas_call(
        _cross_kernel,
        grid=(B // bt, L // tile),
        in_specs=[
            pl.BlockSpec((bt, tile, D), lambda b, j: (b, j, 0)),
            _const_spec((1, D)),
            _const_spec((D, D)),
            _const_spec((D, D)),
            mem_spec,
            mem_spec,
        ],
        out_specs=pl.BlockSpec((bt, tile, D), lambda b, j: (b, j, 0)),
        out_shape=jax.ShapeDtypeStruct((B, L, D), F32),
        scratch_shapes=[pltpu.VMEM((bt, n_mem, D), BF16), pltpu.VMEM((bt, n_mem, D), BF16),
                        pltpu.VMEM((bt * tile, D), BF16)],
        compiler_params=_cparams(),
        name="cross_attn",
    )(x, g.reshape(1, D), w_q, w_o, mem_k, mem_v)


def _memkv_kernel(m_ref, g_ref, wk_ref, wv_ref, k_ref, v_ref):
    bb, n_mem, D = m_ref.shape
    dh = k_ref.shape[4]
    mn = _rms(m_ref[...].reshape(bb * n_mem, D), g_ref[0]).astype(BF16)
    k = jnp.dot(mn, wk_ref[0], preferred_element_type=F32)
    v = jnp.dot(mn, wv_ref[0], preferred_element_type=F32)
    for bi in range(bb):
        rows = slice(bi * n_mem, (bi + 1) * n_mem)
        for hd in range(N_MEM_HEADS):
            k_ref[0, bi, :, hd, :] = k[rows, hd * dh:(hd + 1) * dh]
            v_ref[0, bi, :, hd, :] = v[rows, hd * dh:(hd + 1) * dh]


def _mem_kv(mem, g, w_k, w_v, bb):
    depth, D = g.shape
    B, n_mem, _ = mem.shape
    dh = D // N_MEM_HEADS
    out = jax.ShapeDtypeStruct((depth, B, n_mem, N_MEM_HEADS, dh), F32)
    out_spec = pl.BlockSpec((1, bb, n_mem, N_MEM_HEADS, dh), lambda l, j: (l, j, 0, 0, 0))
    return pl.pallas_call(
        _memkv_kernel,
        grid=(depth, B // bb),
        in_specs=[
            pl.BlockSpec((bb, n_mem, D), lambda l, j: (j, 0, 0)),
            pl.BlockSpec((1, 1, D), lambda l, j: (l, 0, 0)),
            pl.BlockSpec((1, D, D), lambda l, j: (l, 0, 0)),
            pl.BlockSpec((1, D, D), lambda l, j: (l, 0, 0)),
        ],
        out_specs=[out_spec, out_spec],
        out_shape=[out, out],
        compiler_params=_cparams(),
        name="mem_kv",
    )(mem, g.reshape(depth, 1, D), w_k, w_v)


def _even_kernel(x_ref, g_ref, win_ref, pbd_ref, pscale_ref, cw_ref, cb_ref, wax_ref, ba_ref, bx_ref,
                 lam_ref, wout_ref, pool0_ref, conv0_ref, lru0_ref,
                 o_ref, pool_o, conv_o, lru_o,
                 ubuf, xbuf, a_s, b_s, h_s, p_s, hcar, cat_scr, *, pos0, pitch):
    j = pl.program_id(1)
    tl = x_ref.shape[1]
    dp = pbd_ref.shape[0] * pbd_ref.shape[1]
    dr = cw_ref.shape[1]
    n_slab = dr // LANES

    @pl.when(j == 0)
    def _():
        ubuf[0:POOL_HIST, :] = pool0_ref[0]
        xbuf[0:CONV_HIST, :] = conv0_ref[0]
        hcar[...] = lru0_ref[0]
        a_s[:, tl:, :] = jnp.zeros((n_slab, SUBLANES * pitch - tl, LANES), F32)
        b_s[:, tl:, :] = jnp.zeros((n_slab, SUBLANES * pitch - tl, LANES), F32)

    x = x_ref[0]
    h = _rms(x, g_ref[...]).astype(BF16)
    z = jnp.dot(h, win_ref[...], preferred_element_type=F32)
    u_pool = z[:, :dp]
    u_x = z[:, dp:dp + dr]
    u_gate = z[:, dp + dr:]
    ubuf[POOL_HIST:POOL_HIST + tl, :] = u_pool
    xbuf[CONV_HIST:CONV_HIST + tl, :] = u_x

    pos = pos0 + j * tl + lax.broadcasted_iota(jnp.int32, (tl, 1), 0)
    group = dp // len(POOL_WINDOWS)
    for gi, win in enumerate(POOL_WINDOWS):
        cols = slice(gi * group, (gi + 1) * group)
        s = u_pool[:, cols]
        for k in range(1, win):
            s = s + ubuf[POOL_HIST - k:POOL_HIST - k + tl, cols]
        inv_cnt = 1.0 / jnp.minimum(pos + 1, win).astype(F32)
        cat_scr[:, cols] = (s * inv_cnt - u_pool[:, cols]).astype(BF16)
    half = MXU_DIM
    for hf in range(dp // half):
        cols = slice(hf * half, (hf + 1) * half)
        yp = jnp.dot(cat_scr[:, cols], pbd_ref[hf], preferred_element_type=F32)
        cat_scr[:, cols] = (yp * pscale_ref[:, cols]).astype(BF16)

    xc = cb_ref[...] + u_x * cw_ref[CONV_WIDTH - 1:CONV_WIDTH, :]
    for k in range(CONV_WIDTH - 1):
        sh = CONV_WIDTH - 1 - k
        xc = xc + xbuf[CONV_HIST - sh:CONV_HIST - sh + tl, :] * cw_ref[k:k + 1, :]

    xcb = xc.astype(BF16)
    c_lam = -RG_C * jnp.log1p(jnp.exp(-lam_ref[...]))
    for hf in range(dr // half):
        cols = slice(hf * half, (hf + 1) * half)
        gates = jnp.dot(xcb[:, cols], wax_ref[hf], preferred_element_type=F32)
        r_gate = jax.nn.sigmoid(gates[:, :half] + ba_ref[:, cols])
        i_gate = jax.nn.sigmoid(gates[:, half:] + bx_ref[:, cols])
        log_a = r_gate * c_lam[:, cols]
        a = jnp.exp(log_a)
        mult = jnp.sqrt(-jnp.tanh(log_a) * (a * a + 1.0))
        bb = mult * i_gate * xc[:, cols]
        for sl in range(half // LANES):
            k = hf * (half // LANES) + sl
            a_s[k, 0:tl, :] = a[:, sl * LANES:(sl + 1) * LANES]
            b_s[k, 0:tl, :] = bb[:, sl * LANES:(sl + 1) * LANES]

    for k in range(n_slab):
        a_k, b_k, h_k, p_k = a_s.at[k], b_s.at[k], h_s.at[k], p_s.at[k]
        hv = jnp.zeros((SUBLANES, LANES), F32)
        pv = jnp.ones((SUBLANES, LANES), F32)
        for i in range(pitch):
            rows = pl.ds(i, SUBLANES, stride=pitch)
            av = a_k[rows, :]
            hv = av * hv + b_k[rows, :]
            pv = av * pv
            h_k[rows, :] = hv
            p_k[rows, :] = pv
        carry = [hcar[:, k * LANES:(k + 1) * LANES]]
        for r in range(1, SUBLANES):
            carry.append(hv[r - 1:r, :] + pv[r - 1:r, :] * carry[-1])
        cv = jnp.concatenate(carry, axis=0)
        for i in range(pitch):
            rows = pl.ds(i, SUBLANES, stride=pitch)
            h_k[rows, :] = h_k[rows, :] + p_k[rows, :] * cv
        hcar[:, k * LANES:(k + 1) * LANES] = h_k[tl - 1:tl, :]

    hr = jnp.concatenate([h_s[k, 0:tl, :] for k in range(n_slab)], axis=1)
    cat_scr[:, dp:] = (hr * jax.nn.gelu(u_gate)).astype(BF16)
    o_ref[0] = x + jnp.dot(cat_scr[...], wout_ref[...], preferred_element_type=F32)

    pool_o[0] = u_pool[tl - POOL_HIST:, :]
    conv_o[0] = u_x[tl - CONV_HIST:, :]
    lru_o[0] = hr[tl - SUBLANES:, :]
    ubuf[0:POOL_HIST, :] = u_pool[tl - POOL_HIST:, :]
    xbuf[0:CONV_HIST, :] = u_x[tl - CONV_HIST:, :]


def _even_layer(x, g, wts, pool0, conv0, lru0, pos0, tile):
    B, L, D = x.shape
    dp = pool0.shape[2]
    dr = conv0.shape[2]
    pitch = _scan_pitch(tile)
    n_slab = dr // LANES
    row = lambda v: v.reshape(1, -1)
    consts = [row(g), wts["w_in"], wts["pool_bd"], row(wts["pool_scale"]), wts["conv_w"],
              row(wts["conv_b"]), wts["w_ax"], row(wts["b_a"]), row(wts["b_x"]),
              row(wts["lam"]), wts["w_out"]]
    in_specs = [pl.BlockSpec((1, tile, D), lambda b, j: (b, j, 0))]
    in_specs += [_const_spec(c.shape) for c in consts]
    in_specs += [
        pl.BlockSpec((1, POOL_HIST, dp), lambda b, j: (b, 0, 0)),
        pl.BlockSpec((1, CONV_HIST, dr), lambda b, j: (b, 0, 0)),
        pl.BlockSpec((1, 1, dr), lambda b, j: (b, 0, 0)),
    ]
    slab = pltpu.VMEM((n_slab, SUBLANES * pitch, LANES), F32)
    return pl.pallas_call(
        functools.partial(_even_kernel, pos0=pos0, pitch=pitch),
        grid=(B, L // tile),
        in_specs=in_specs,
        out_specs=[
            pl.BlockSpec((1, tile, D), lambda b, j: (b, j, 0)),
            pl.BlockSpec((1, POOL_HIST, dp), lambda b, j: (b, 0, 0)),
            pl.BlockSpec((1, CONV_HIST, dr), lambda b, j: (b, 0, 0)),
            pl.BlockSpec((1, SUBLANES, dr), lambda b, j: (b, 0, 0)),
        ],
        out_shape=[
            jax.ShapeDtypeStruct((B, L, D), F32),
            jax.ShapeDtypeStruct((B, POOL_HIST, dp), F32),
            jax.ShapeDtypeStruct((B, CONV_HIST, dr), F32),
            jax.ShapeDtypeStruct((B, SUBLANES, dr), F32),
        ],
        scratch_shapes=[
            pltpu.VMEM((POOL_HIST + tile, dp), F32),
            pltpu.VMEM((CONV_HIST + tile, dr), F32),
            slab, slab, slab, slab,
            pltpu.VMEM((1, dr), F32),
            pltpu.VMEM((tile, dp + dr), BF16),
        ],
        compiler_params=_cparams(),
        name="even_mixer",
    )(x, *consts, pool0, conv0, lru0)


def _odd_kernel(x_ref, g_ref, wqkv_ref, wo_ref, sink_ref, ones_ref, kprev_ref, vprev_ref,
                o_ref, ktail_ref, vtail_ref,
                q_scr, k_buf, v_buf, att_scr, *, prefix_valid):
    j = pl.program_id(1)
    tq = x_ref.shape[1]
    nq = q_scr.shape[1]
    nkv = k_buf.shape[1]
    group = nq // nkv
    n_chunks = tq // CHUNK

    @pl.when(j == 0)
    def _():
        pad = jnp.zeros((KV_HIST - WINDOW, nkv), BF16)
        k_buf[0:KV_HIST - WINDOW, :] = pad
        v_buf[0:KV_HIST - WINDOW, :] = pad
        k_buf[KV_HIST - WINDOW:KV_HIST, :] = kprev_ref[0].astype(BF16)
        v_buf[KV_HIST - WINDOW:KV_HIST, :] = vprev_ref[0].astype(BF16)

    @pl.when(j > 0)
    def _():
        k_buf[0:KV_HIST, :] = k_buf[tq:tq + KV_HIST, :]
        v_buf[0:KV_HIST, :] = v_buf[tq:tq + KV_HIST, :]

    x = x_ref[0]
    h = _rms(x, g_ref[...]).astype(BF16)
    z = jnp.dot(h, wqkv_ref[...], preferred_element_type=F32)
    q_scr[...] = (z[:, :nq] * (HEAD_DIM ** -0.5)).astype(BF16)
    k_new = z[:, nq:nq + nkv]
    v_new = z[:, nq + nkv:]
    k_buf[KV_HIST:KV_HIST + tq, :] = k_new.astype(BF16)
    v_buf[KV_HIST:KV_HIST + tq, :] = v_new.astype(BF16)
    if tq >= WINDOW:
        ktail_ref[0] = k_new[tq - WINDOW:, :]
        vtail_ref[0] = v_new[tq - WINDOW:, :]
    else:
        ktail_ref[0] = jnp.concatenate([kprev_ref[0][tq:, :], k_new], axis=0)
        vtail_ref[0] = jnp.concatenate([vprev_ref[0][tq:, :], v_new], axis=0)

    lane = lax.broadcasted_iota(jnp.int32, (1, nkv), 1)
    head_masks = [(lane // HEAD_DIM) == kv for kv in range(N_KV_HEADS)]
    key_idx = lax.broadcasted_iota(jnp.int32, (1, KEY_SPAN), 1)

    for c in range(n_chunks):
        r0 = c * CHUNK
        qs = jnp.concatenate([q_scr[r0:r0 + CHUNK, gi * nkv:(gi + 1) * nkv] for gi in range(group)],
                             axis=0)
        kspan = k_buf[r0:r0 + KEY_SPAN, :]
        vspan = v_buf[r0:r0 + KEY_SPAN, :]
        zero = jnp.zeros_like(kspan)
        kbd = jnp.concatenate([jnp.where(m, kspan, zero) for m in head_masks], axis=0)
        vbd = jnp.concatenate([jnp.where(m, vspan, zero) for m in head_masks], axis=0)
        s_all = lax.dot_general(qs, kbd, (((1,), (1,)), ((), ())), preferred_element_type=F32)
        key_pos = (j * tq - KV_HIST) + r0 + key_idx
        valid = key_idx >= (KEY_SPAN - WINDOW - CHUNK)
        if not prefix_valid:
            valid = valid & (key_pos >= 0)
        ps, mxs = [], []
        for kv in range(N_KV_HEADS):
            s = jnp.where(valid, s_all[:, kv * KEY_SPAN:(kv + 1) * KEY_SPAN], NEG_INF)
            mx = jnp.max(s, axis=-1, keepdims=True)
            ps.append(jnp.exp(s - mx).astype(BF16))
            mxs.append(mx)
        o_all = jnp.dot(jnp.concatenate(ps, axis=1), jnp.concatenate([vbd, ones_ref[...]], axis=1),
                        preferred_element_type=F32)
        mx_all = mxs[N_KV_HEADS - 1]
        for kv in range(N_KV_HEADS - 2, -1, -1):
            mx_all = jnp.where(head_masks[kv], mxs[kv], mx_all)
        den = o_all[:, nkv:] + jnp.exp(sink_ref[...] - mx_all)
        o = (o_all[:, :nkv] * (1.0 / den)).astype(BF16)
        for gi in range(group):
            att_scr[r0:r0 + CHUNK, gi * nkv:(gi + 1) * nkv] = o[gi * CHUNK:(gi + 1) * CHUNK, :]

    o_ref[0] = x + jnp.dot(att_scr[...], wo_ref[...], preferred_element_type=F32)


def _odd_layer(x, g, w_qkv, w_o, sink_tab, k_prev, v_prev, prefix_valid, tile):
    B, L, D = x.shape
    nkv = k_prev.shape[2]
    nq = w_o.shape[0]
    head_ones = (jnp.arange(N_KV_HEADS * KEY_SPAN)[:, None] // KEY_SPAN
                 == jnp.arange(nkv)[None, :] // HEAD_DIM).astype(BF16)
    tail = jax.ShapeDtypeStruct((B, WINDOW, nkv), F32)
    return pl.pallas_call(
        functools.partial(_odd_kernel, prefix_valid=prefix_valid),
        grid=(B, L // tile),
        in_specs=[
            pl.BlockSpec((1, tile, D), lambda b, j: (b, j, 0)),
            _const_spec((1, D)),
            _const_spec(w_qkv.shape),
            _const_spec(w_o.shape),
            _const_spec(sink_tab.shape),
            _const_spec(head_ones.shape),
            pl.BlockSpec((1, WINDOW, nkv), lambda b, j: (b, 0, 0)),
            pl.BlockSpec((1, WINDOW, nkv), lambda b, j: (b, 0, 0)),
        ],
        out_specs=[
            pl.BlockSpec((1, tile, D), lambda b, j: (b, j, 0)),
            pl.BlockSpec((1, WINDOW, nkv), lambda b, j: (b, 0, 0)),
            pl.BlockSpec((1, WINDOW, nkv), lambda b, j: (b, 0, 0)),
        ],
        out_shape=[jax.ShapeDtypeStruct((B, L, D), F32), tail, tail],
        scratch_shapes=[
            pltpu.VMEM((tile, nq), BF16),
            pltpu.VMEM((KV_HIST + tile, nkv), BF16),
            pltpu.VMEM((KV_HIST + tile, nkv), BF16),
            pltpu.VMEM((tile, nq), BF16),
        ],
        compiler_params=_cparams(),
        name="swa_mixer",
    )(x, g.reshape(1, D), w_qkv, w_o, sink_tab, head_ones, k_prev, v_prev)


def _block_diag(blocks):
    n, r, c = blocks.shape
    eye = jnp.eye(n, dtype=blocks.dtype)
    return (eye[:, None, :, None] * blocks[:, :, None, :]).reshape(n * r, n * c)


def _prep_even(p, e):
    pool_w = p["pool_w"][e]
    per_tile = MXU_DIM // pool_w.shape[1]
    pool_bd = jnp.stack([_block_diag(pool_w[i:i + per_tile])
                         for i in range(0, pool_w.shape[0], per_tile)])
    wa, wx = p["w_rg_a"][e], p["w_rg_x"][e]
    per_tile = MXU_DIM // wa.shape[1]
    w_ax = jnp.stack([jnp.concatenate([_block_diag(wa[i:i + per_tile]), _block_diag(wx[i:i + per_tile])],
                                      axis=1)
                      for i in range(0, wa.shape[0], per_tile)])
    return dict(
        w_in=p["w_in_even"][e].astype(BF16), pool_bd=pool_bd.astype(BF16), pool_scale=p["pool_scale"][e],
        conv_w=p["conv_w"][e], conv_b=p["conv_b"][e], w_ax=w_ax.astype(BF16), b_a=p["b_rg_a"][e],
        b_x=p["b_rg_x"][e], lam=p["rg_lambda"][e], w_out=p["w_out_even"][e].astype(BF16))


def _prep_odd(p, o):
    w_qkv = p["w_qkv_odd"][o]
    D = w_qkv.shape[0]
    w_o = p["w_o_odd"][o]
    nq = w_o.shape[0]
    group = nq // (N_KV_HEADS * HEAD_DIM)
    wq = w_qkv[:, :nq].reshape(D, N_KV_HEADS, group, HEAD_DIM).transpose(0, 2, 1, 3).reshape(D, nq)
    w_qkv = jnp.concatenate([wq, w_qkv[:, nq:]], axis=1)
    w_o = w_o.reshape(N_KV_HEADS, group, HEAD_DIM, -1).transpose(1, 0, 2, 3).reshape(nq, -1)
    sinks = p["attn_sinks"][o].reshape(N_KV_HEADS, group).T
    sink_tab = jnp.repeat(jnp.repeat(sinks, CHUNK, axis=0), HEAD_DIM, axis=1).astype(F32)
    return dict(w_qkv=w_qkv.astype(BF16), w_o=w_o.astype(BF16), sink_tab=sink_tab)


def _pad_rows(a, rows):
    return jnp.pad(a, ((0, 0), (rows - a.shape[1], 0), (0, 0)))


def _trunk(x, pos0, prefix_valid, pool_st, conv_st, lru_st, swa_k, swa_v, mem_k, mem_v, p, prep,
           tile, tile_mix):
    depth = p["norm_mix"].shape[0]
    B = x.shape[0]
    new_pool, new_conv, new_lru, new_k, new_v = [], [], [], [], []
    for layer in range(depth):
        if layer % 2 == 0:
            e = layer // 2
            x, pn, cn, ln = _even_layer(
                x, p["norm_mix"][layer], prep["even"][e],
                _pad_rows(pool_st[e], POOL_HIST), _pad_rows(conv_st[e], CONV_HIST),
                lru_st[e][:, None, :], pos0, tile_mix)
            new_pool.append(pn[:, POOL_HIST - pool_st.shape[2]:])
            new_conv.append(cn[:, CONV_HIST - conv_st.shape[2]:])
            new_lru.append(ln[:, SUBLANES - 1])
        else:
            o = layer // 2
            kp = swa_k[o].reshape(B, WINDOW, -1)
            vp = swa_v[o].reshape(B, WINDOW, -1)
            x, kn, vn = _odd_layer(x, p["norm_mix"][layer], prep["odd"][o]["w_qkv"], prep["odd"][o]["w_o"],
                                   prep["odd"][o]["sink_tab"], kp, vp, prefix_valid, tile_mix)
            new_k.append(kn.reshape(swa_k[o].shape))
            new_v.append(vn.reshape(swa_v[o].shape))
        L = x.shape[1]
        bt = max(1, tile // L)
        x = _cross(x, p["norm_cross"][layer], prep["w_mq"][layer], prep["w_mo"][layer],
                   mem_k, mem_v, layer, min(tile, L), min(bt, CROSS_MAX_BATCH_BLOCK, B))
        g_final = p["norm_final"] if layer == depth - 1 else None
        mlp_w = (p["norm_mlp"][layer], prep["w_up"][layer], prep["w_down"][layer], g_final)
        if L >= tile:
            x = _mlp(x, *mlp_w, tile)
        else:
            x = _mlp(x.reshape(1, B * L, -1), *mlp_w, min(tile, B * L)).reshape(B, L, -1)
    return (x, jnp.stack(new_pool), jnp.stack(new_conv), jnp.stack(new_lru),
            jnp.stack(new_k), jnp.stack(new_v))


def kernel(x_prompt, x_sample, state_pool, state_conv, state_lru, cache_swa_k, cache_swa_v,
           cache_mem_k, cache_mem_v, mem_prompt, norm_mix, norm_cross, norm_mem, norm_mlp, norm_final,
           w_in_even, conv_w, conv_b, w_rg_a, b_rg_a, w_rg_x, b_rg_x, rg_lambda, pool_w, pool_scale,
           w_out_even, w_qkv_odd, attn_sinks, w_o_odd, w_mq, w_mk, w_mv, w_mo, w_up, w_down):
    p = dict(norm_mix=norm_mix, norm_cross=norm_cross, norm_mlp=norm_mlp, norm_final=norm_final,
             w_in_even=w_in_even, conv_w=conv_w, conv_b=conv_b, w_rg_a=w_rg_a, b_rg_a=b_rg_a,
             w_rg_x=w_rg_x, b_rg_x=b_rg_x, rg_lambda=rg_lambda, pool_w=pool_w, pool_scale=pool_scale,
             w_out_even=w_out_even, w_qkv_odd=w_qkv_odd, attn_sinks=attn_sinks, w_o_odd=w_o_odd)
    depth = norm_mix.shape[0]
    n_even, n_odd = state_pool.shape[0], cache_swa_k.shape[0]
    prep = dict(
        even=[_prep_even(p, e) for e in range(n_even)],
        odd=[_prep_odd(p, o) for o in range(n_odd)],
        w_mq=w_mq.astype(BF16), w_mo=w_mo.astype(BF16),
        w_up=w_up.astype(BF16), w_down=w_down.astype(BF16))

    B, L, D = x_prompt.shape
    dt = x_prompt.dtype
    mem_k_p, mem_v_p = _mem_kv(mem_prompt, norm_mem, w_mk.astype(BF16), w_mv.astype(BF16), bb=2)

    zero_pool = jnp.zeros((n_even, B) + state_pool.shape[2:], dt)
    zero_conv = jnp.zeros((n_even, B) + state_conv.shape[2:], dt)
    zero_lru = jnp.zeros((n_even, B) + state_lru.shape[2:], dt)
    zero_kv = jnp.zeros((n_odd, B) + cache_swa_k.shape[2:], dt)
    y_prompt, pool_p, conv_p, lru_p, swa_k_p, swa_v_p = _trunk(
        x_prompt, 0, False, zero_pool, zero_conv, zero_lru, zero_kv, zero_kv, mem_k_p, mem_v_p, p, prep,
        tile=512, tile_mix=256)
    Ls = x_sample.shape[1]
    y_sample, pool_s, conv_s, lru_s, swa_k_s, swa_v_s = _trunk(
        x_sample, PAST_LEN, True, state_pool, state_conv, state_lru, cache_swa_k, cache_swa_v,
        cache_mem_k, cache_mem_v, p, prep, tile=512, tile_mix=Ls)
    return (y_prompt, y_sample, pool_p, conv_p, lru_p, swa_k_p, swa_v_p, mem_k_p, mem_v_p,
            pool_s, conv_s, lru_s, swa_k_s, swa_v_s)
```

```python
import functools

import jax
import jax.numpy as jnp
from jax import lax
from jax.experimental import pallas as pl
from jax.experimental.pallas import tpu as pltpu

F32 = jnp.float32
BF16 = jnp.bfloat16

EPS = 1e-6
NEG_INF = -1e30
CHUNK = 64
POOL_WINDOWS = (2, 4, 8, 16)
N_RNN_BLOCKS = 8
CONV_WIDTH = 4
RG_C = 8.0
HEAD_DIM = 64
N_KV_HEADS = 4
WINDOW = 128
N_MEM_HEADS = 4

LANES = 128
SUBLANES = 8
MXU_DIM = 256
VMEM_LIMIT_BYTES = 56 * 1024 * 1024

POOL_HIST = 16
CONV_HIST = 8
KEY_SPAN = WINDOW + 2 * CHUNK
KV_HIST = KEY_SPAN - CHUNK
CROSS_MAX_BATCH_BLOCK = 4
PAST_LEN = 2048
EVEN_SUBTILES = 2


def _cparams():
    return pltpu.CompilerParams(
        dimension_semantics=("arbitrary", "arbitrary"),
        vmem_limit_bytes=VMEM_LIMIT_BYTES)


def _const_spec(shape):
    zeros = (0,) * len(shape)
    return pl.BlockSpec(shape, lambda *_: zeros, pipeline_mode=pl.Buffered(1))


def _rms(x, g):
    ms = jnp.mean(x * x, axis=-1, keepdims=True)
    return x * lax.rsqrt(ms + EPS) * g


LOG2E = 1.4426950408889634
F32_TINY = 1.1754943508222875e-38
GELU_C0 = -2.0 * 0.7978845608028654 * LOG2E
GELU_C1 = GELU_C0 * 0.044715


def _sigmoid(v):
    return 1.0 / (1.0 + jnp.exp2(v * (-LOG2E)))


def _gelu_tanh(v):
    return v / (1.0 + jnp.exp2(v * (GELU_C1 * (v * v) + GELU_C0)))


def _scan_pitch(tile):
    p = max(-(-tile // SUBLANES), POOL_HIST)
    while p % 8 != 4:
        p += 1
    return p


def _mlp_kernel(x_ref, g_ref, wu_ref, wd_ref, *rest, final_norm, ff_chunk):
    if final_norm:
        gf_ref, o_ref, a_scr = rest
    else:
        o_ref, a_scr = rest
    x = x_ref[0]
    h = _rms(x, g_ref[...]).astype(BF16)
    d_ff = wu_ref.shape[1]
    for c in range(d_ff // ff_chunk):
        cols = slice(c * ff_chunk, (c + 1) * ff_chunk)
        u = jnp.dot(h, wu_ref[:, cols], preferred_element_type=F32)
        r = jnp.maximum(u, 0.0)
        a_scr[:, cols] = (r * r).astype(BF16)
    y = x + jnp.dot(a_scr[...], wd_ref[...], preferred_element_type=F32)
    if final_norm:
        y = _rms(y, gf_ref[...])
    o_ref[0] = y


def _mlp(x, g, w_up, w_down, g_final, tile):
    B, L, D = x.shape
    d_ff = w_up.shape[1]
    final_norm = g_final is not None
    in_specs = [
        pl.BlockSpec((1, tile, D), lambda b, j: (b, j, 0)),
        _const_spec((1, D)),
        _const_spec((D, d_ff)),
        _const_spec((d_ff, D)),
    ]
    args = [x, g.reshape(1, D), w_up, w_down]
    if final_norm:
        in_specs.append(_const_spec((1, D)))
        args.append(g_final.reshape(1, D))
    return pl.pallas_call(
        functools.partial(_mlp_kernel, final_norm=final_norm, ff_chunk=4 * MXU_DIM),
        grid=(B, L // tile),
        in_specs=in_specs,
        out_specs=pl.BlockSpec((1, tile, D), lambda b, j: (b, j, 0)),
        out_shape=jax.ShapeDtypeStruct((B, L, D), F32),
        scratch_shapes=[pltpu.VMEM((tile, d_ff), BF16)],
        compiler_params=_cparams(),
        name="mlp",
    )(*args)


def _cross_kernel(x_ref, g_ref, wq_ref, wo_ref, mk_ref, mv_ref, o_ref, att_scr, *kv_scr):
    bt, tile, D = x_ref.shape
    dh = D // N_MEM_HEADS
    if kv_scr:
        k_src, v_src = kv_scr

        @pl.when(pl.program_id(1) == 0)
        def _():
            for bi in range(bt):
                for hd in range(N_MEM_HEADS):
                    cols = slice(hd * dh, (hd + 1) * dh)
                    k_src[0, bi, :, cols] = mk_ref[0, bi, :, hd, :].astype(BF16)
                    v_src[0, bi, :, cols] = mv_ref[0, bi, :, hd, :].astype(BF16)
    else:
        k_src, v_src = mk_ref, mv_ref

    x = x_ref[...].reshape(bt * tile, D)
    h = _rms(x, g_ref[...]).astype(BF16)
    q = jnp.dot(h, wq_ref[...], preferred_element_type=F32)
    q = (q * (dh ** -0.5)).astype(BF16)
    for bi in range(bt):
        rows = slice(bi * tile, (bi + 1) * tile)
        for hd in range(N_MEM_HEADS):
            cols = slice(hd * dh, (hd + 1) * dh)
            s = lax.dot_general(q[rows, cols], k_src[0, bi, :, cols], (((1,), (1,)), ((), ())),
                                preferred_element_type=F32)
            mx = jnp.max(s, axis=-1, keepdims=True)
            p = jnp.exp(s - mx)
            den = jnp.sum(p, axis=-1, keepdims=True)
            o = jnp.dot(p.astype(BF16), v_src[0, bi, :, cols], preferred_element_type=F32)
            att_scr[rows, cols] = (o * (1.0 / den)).astype(BF16)
    y = x + jnp.dot(att_scr[...], wo_ref[...], preferred_element_type=F32)
    o_ref[...] = y.reshape(bt, tile, D)


def _cross(x, g, w_q, w_o, mem_k, mem_v, layer, tile, bt):
    B, L, D = x.shape
    n_mem = mem_k.shape[2]
    zeros = (0,) * (mem_k.ndim - 2)
    mem_spec = pl.BlockSpec((1, bt) + mem_k.shape[2:], lambda b, j: (layer, b) + zeros)
    scratch = [pltpu.VMEM((bt * tile, D), BF16)]
    if mem_k.ndim == 5:
        scratch += [pltpu.VMEM((1, bt, n_mem, D), BF16)] * 2
    return pl.pallas_call(
        _cross_kernel,
        grid=(B // bt, L // tile),
        in_specs=[
            pl.BlockSpec((bt, tile, D), lambda b, j: (b, j, 0)),
            _const_spec((1, D)),
            _const_spec((D, D)),
            _const_spec((D, D)),
            mem_spec,
            mem_spec,
        ],
        out_specs=pl.BlockSpec((bt, tile, D), lambda b, j: (b, j, 0)),
        out_shape=jax.ShapeDtypeStruct((B, L, D), F32),
        scratch_shapes=scratch,
        compiler_params=_cparams(),
        name="cross_attn",
    )(x, g.reshape(1, D), w_q, w_o, mem_k, mem_v)


def _memkv_kernel(m_ref, g_ref, wk_ref, wv_ref, k_ref, v_ref, kp_ref, vp_ref):
    bb, n_mem, D = m_ref.shape
    dh = k_ref.shape[4]
    mn = _rms(m_ref[...].reshape(bb * n_mem, D), g_ref[0]).astype(BF16)
    k = jnp.dot(mn, wk_ref[0], preferred_element_type=F32)
    v = jnp.dot(mn, wv_ref[0], preferred_element_type=F32)
    kp_ref[0] = k.astype(BF16).reshape(bb, n_mem, D)
    vp_ref[0] = v.astype(BF16).reshape(bb, n_mem, D)
    for bi in range(bb):
        rows = slice(bi * n_mem, (bi + 1) * n_mem)
        for hd in range(N_MEM_HEADS):
            k_ref[0, bi, :, hd, :] = k[rows, hd * dh:(hd + 1) * dh]
            v_ref[0, bi, :, hd, :] = v[rows, hd * dh:(hd + 1) * dh]


def _mem_kv(mem, g, w_k, w_v, bb):
    depth, D = g.shape
    B, n_mem, _ = mem.shape
    dh = D // N_MEM_HEADS
    out = jax.ShapeDtypeStruct((depth, B, n_mem, N_MEM_HEADS, dh), F32)
    out_spec = pl.BlockSpec((1, bb, n_mem, N_MEM_HEADS, dh), lambda l, j: (l, j, 0, 0, 0))
    packed = jax.ShapeDtypeStruct((depth, B, n_mem, D), BF16)
    packed_spec = pl.BlockSpec((1, bb, n_mem, D), lambda l, j: (l, j, 0, 0))
    return pl.pallas_call(
        _memkv_kernel,
        grid=(depth, B // bb),
        in_specs=[
            pl.BlockSpec((bb, n_mem, D), lambda l, j: (j, 0, 0)),
            pl.BlockSpec((1, 1, D), lambda l, j: (l, 0, 0)),
            pl.BlockSpec((1, D, D), lambda l, j: (l, 0, 0)),
            pl.BlockSpec((1, D, D), lambda l, j: (l, 0, 0)),
        ],
        out_specs=[out_spec, out_spec, packed_spec, packed_spec],
        out_shape=[out, out, packed, packed],
        compiler_params=_cparams(),
        name="mem_kv",
    )(mem, g.reshape(depth, 1, D), w_k, w_v)


def _even_kernel(x_ref, g_ref, win_ref, pbd_ref, pscale_ref, cw_ref, cb_ref, wax_ref, ba_ref, bx_ref,
                 lam_ref, wout_ref, pool0_ref, conv0_ref, lru0_ref,
                 o_ref, pool_o, conv_o, lru_o,
                 hcar, *sub_scratch, pos0, pitch):
    j = pl.program_id(1)
    n_sub = len(sub_scratch) // 2
    zs_list, ys_list = sub_scratch[:n_sub], sub_scratch[n_sub:]
    n_in = zs_list[0].shape[0]
    tl = x_ref.shape[1] // n_sub
    dp = pbd_ref.shape[0] * pbd_ref.shape[1]
    dr = cw_ref.shape[1]
    n_pool = dp // LANES
    n_rnn = dr // LANES
    rows_p = SUBLANES * pitch
    half = MXU_DIM
    per_half = half // LANES

    @pl.when(j == 0)
    def _():
        zs_tail = zs_list[n_sub - 1]
        for k in range(n_pool):
            zs_tail[k, tl - POOL_HIST:tl, :] = pool0_ref[0, :, k * LANES:(k + 1) * LANES]
        for k in range(n_rnn):
            zs_tail[n_pool + k, tl - CONV_HIST:tl, :] = conv0_ref[0, :, k * LANES:(k + 1) * LANES]
        hcar[...] = lru0_ref[0]
        for zs in zs_list:
            zs[:, tl:, :] = jnp.zeros((n_in, rows_p - tl, LANES), F32)

    subs = [_even_subtile(s, n_sub, tl, x_ref, g_ref, win_ref, pbd_ref, pscale_ref, cw_ref, cb_ref, wax_ref,
                          ba_ref, bx_ref, lam_ref, wout_ref, o_ref, pool_o, conv_o, lru_o,
                          zs_list[(s - 1) % n_sub], zs_list[s], ys_list[s], hcar,
                          pos0 + (j * n_sub + s) * tl, pitch) for s in range(n_sub)]
    next(subs[0])
    for s in range(n_sub):
        next(subs[s])
        if s + 1 < n_sub:
            next(subs[s + 1])
        for _ in subs[s]:
            pass


def _even_subtile(s, n_sub, tl, x_ref, g_ref, win_ref, pbd_ref, pscale_ref, cw_ref, cb_ref, wax_ref,
                  ba_ref, bx_ref, lam_ref, wout_ref, o_ref, pool_o, conv_o, lru_o,
                  zs_prev, zs, ys, hcar, pos_tile, pitch):
    dp = pbd_ref.shape[0] * pbd_ref.shape[1]
    dr = cw_ref.shape[1]
    n_pool = dp // LANES
    n_rnn = dr // LANES
    n_in = zs.shape[0]
    half = MXU_DIM
    per_half = half // LANES
    rows = slice(s * tl, (s + 1) * tl)
    is_last = s == n_sub - 1

    hist_pool = [zs_prev[k, tl - POOL_HIST:tl, :] for k in range(n_pool)]
    hist_x = [zs_prev[n_pool + k, tl - CONV_HIST:tl, :] for k in range(n_rnn)]

    x = x_ref[0, rows, :]
    h = _rms(x, g_ref[...]).astype(BF16)
    z = jnp.dot(h, win_ref[...], preferred_element_type=F32)
    for k in range(n_in):
        zs[k, 0:tl, :] = z[:, k * LANES:(k + 1) * LANES]
    if is_last:
        pool_o[0] = z[tl - POOL_HIST:, :dp]
        conv_o[0] = z[tl - CONV_HIST:, dp:dp + dr]
    yield

    def gather(k):
        return [zs.at[k][pl.ds(i, SUBLANES, stride=pitch), :] for i in range(pitch)]

    def stack(vregs):
        return jnp.concatenate(vregs, axis=0)

    def unstack(arr, lane_slab):
        return [arr[i * SUBLANES:(i + 1) * SUBLANES, lane_slab * LANES:(lane_slab + 1) * LANES]
                for i in range(pitch)]

    sub = lax.broadcasted_iota(jnp.int32, (SUBLANES, LANES), 0)
    first_chunk = sub == 0

    def with_history(cur, hist, depth):
        n_hist = hist.shape[0]
        wraps = []
        for m in range(depth, 0, -1):
            prev_end = pltpu.roll(cur[pitch - m], 1, axis=0)
            wraps.append(jnp.where(first_chunk, hist[n_hist - m:n_hist - m + 1, :], prev_end))
        return wraps + cur

    pos_first = pos_tile + sub * pitch
    d_slabs = []
    for gi, win in enumerate(POOL_WINDOWS):
        u = gather(gi)
        level = with_history(u, hist_pool[gi], win - 1)
        span = 1
        while span < win:
            level = [level[n] + level[n - span] for n in range(span, len(level))]
            span *= 2
        d = []
        for i in range(pitch):
            if i >= win - 1:
                inv_cnt = 1.0 / win
            else:
                inv_cnt = 1.0 / jnp.minimum(pos_first + (i + 1), win).astype(F32)
            d.append(level[i] * inv_cnt - u[i])
        d_slabs.append(stack(d))
    y_pool = []
    for hf in range(n_pool // per_half):
        cols = slice(hf * half, (hf + 1) * half)
        dd = jnp.concatenate(d_slabs[hf * per_half:(hf + 1) * per_half], axis=1).astype(BF16)
        yp = jnp.dot(dd, pbd_ref[hf], preferred_element_type=F32)
        y_pool.append((yp * pscale_ref[:, cols]).astype(BF16))

    xc_slabs = []
    for k in range(n_rnn):
        lanes = slice(k * LANES, (k + 1) * LANES)
        ext = with_history(gather(n_pool + k), hist_x[k], CONV_WIDTH - 1)
        taps = [jnp.broadcast_to(cw_ref[t:t + 1, lanes], (SUBLANES, LANES)) for t in range(CONV_WIDTH)]
        bias = jnp.broadcast_to(cb_ref[:, lanes], (SUBLANES, LANES))
        xc = []
        for i in range(pitch):
            acc = bias
            for t in range(CONV_WIDTH):
                acc = acc + ext[i + t] * taps[t]
            xc.append(acc)
        xc_slabs.append(stack(xc))

    c_lam = -RG_C * jnp.log1p(jnp.exp(-lam_ref[...]))
    t_last = tl - 1
    h_slabs, h_last = [], []
    xc_halves = [jnp.concatenate(xc_slabs[hf * per_half:(hf + 1) * per_half], axis=1)
                 for hf in range(n_rnn // per_half)]
    gate_halves = [jnp.dot(xc.astype(BF16), wax_ref[hf], preferred_element_type=F32)
                   for hf, xc in enumerate(xc_halves)]
    yield
    for hf in range(n_rnn // per_half):
        cols = slice(hf * half, (hf + 1) * half)
        xc, gates = xc_halves[hf], gate_halves[hf]
        r_gate = _sigmoid(gates[:, :half] + ba_ref[:, cols])
        i_gate = _sigmoid(gates[:, half:] + bx_ref[:, cols])
        a = jnp.exp2(r_gate * (c_lam[:, cols] * LOG2E))
        one_minus_a2 = jnp.tanh(r_gate * (-c_lam[:, cols])) * (a * a + 1.0)
        mult = one_minus_a2 * lax.rsqrt(jnp.maximum(one_minus_a2, F32_TINY))
        bb = mult * i_gate * xc
        for sl in range(per_half):
            k = hf * per_half + sl
            lanes = slice(k * LANES, (k + 1) * LANES)
            a_v, b_v = unstack(a, sl), unstack(bb, sl)
            hv = jnp.zeros((SUBLANES, LANES), F32)
            pv = jnp.ones((SUBLANES, LANES), F32)
            h_loc, p_loc = [], []
            for i in range(pitch):
                hv = a_v[i] * hv + b_v[i]
                pv = a_v[i] * pv
                h_loc.append(hv)
                p_loc.append(pv)
            carry = [hcar[:, lanes]]
            for r in range(1, SUBLANES):
                carry.append(hv[r - 1:r, :] + pv[r - 1:r, :] * carry[-1])
            cv = jnp.concatenate(carry, axis=0)
            h_fix = [h_loc[i] + p_loc[i] * cv for i in range(pitch)]
            last = h_fix[t_last % pitch][t_last // pitch:t_last // pitch + 1, :]
            hcar[:, lanes] = last
            h_last.append(last)
            h_slabs.append(stack(h_fix))
    if is_last:
        lru_o[0] = jnp.concatenate(h_last, axis=1)

    gate = jnp.concatenate([stack(gather(n_pool + n_rnn + k)) for k in range(n_rnn)], axis=1)
    y_rnn = (jnp.concatenate(h_slabs, axis=1) * _gelu_tanh(gate)).astype(BF16)
    cat = jnp.concatenate(y_pool + [y_rnn], axis=1)
    y = jnp.dot(cat, wout_ref[...], preferred_element_type=F32)
    for k in range(ys.shape[0]):
        for i in range(pitch):
            ys.at[k][pl.ds(i, SUBLANES, stride=pitch), :] = (
                y[i * SUBLANES:(i + 1) * SUBLANES, k * LANES:(k + 1) * LANES])
    o_ref[0, rows, :] = x + jnp.concatenate([ys[k, 0:tl, :] for k in range(ys.shape[0])], axis=1)


def _even_layer(x, g, wts, pool0, conv0, lru0, pos0, tile, n_sub):
    B, L, D = x.shape
    dp = pool0.shape[2]
    dr = conv0.shape[2]
    pitch = _scan_pitch(tile // n_sub)
    row = lambda v: v.reshape(1, -1)
    consts = [row(g), wts["w_in"], wts["pool_bd"], row(wts["pool_scale"]), wts["conv_w"],
              row(wts["conv_b"]), wts["w_ax"], row(wts["b_a"]), row(wts["b_x"]),
              row(wts["lam"]), wts["w_out"]]
    in_specs = [pl.BlockSpec((1, tile, D), lambda b, j: (b, j, 0))]
    in_specs += [_const_spec(c.shape) for c in consts]
    in_specs += [
        pl.BlockSpec((1, POOL_HIST, dp), lambda b, j: (b, 0, 0)),
        pl.BlockSpec((1, CONV_HIST, dr), lambda b, j: (b, 0, 0)),
        pl.BlockSpec((1, 1, dr), lambda b, j: (b, 0, 0)),
    ]
    d_in = wts["w_in"].shape[1]
    return pl.pallas_call(
        functools.partial(_even_kernel, pos0=pos0, pitch=pitch),
        grid=(B, L // tile),
        in_specs=in_specs,
        out_specs=[
            pl.BlockSpec((1, tile, D), lambda b, j: (b, j, 0)),
            pl.BlockSpec((1, POOL_HIST, dp), lambda b, j: (b, 0, 0)),
            pl.BlockSpec((1, CONV_HIST, dr), lambda b, j: (b, 0, 0)),
            pl.BlockSpec((1, 1, dr), lambda b, j: (b, 0, 0)),
        ],
        out_shape=[
            jax.ShapeDtypeStruct((B, L, D), F32),
            jax.ShapeDtypeStruct((B, POOL_HIST, dp), F32),
            jax.ShapeDtypeStruct((B, CONV_HIST, dr), F32),
            jax.ShapeDtypeStruct((B, 1, dr), F32),
        ],
        scratch_shapes=(
            [pltpu.VMEM((1, dr), F32)]
            + [pltpu.VMEM((d_in // LANES, SUBLANES * pitch, LANES), F32)] * n_sub
            + [pltpu.VMEM((D // LANES, SUBLANES * pitch, LANES), F32)] * n_sub),
        compiler_params=_cparams(),
        name="even_mixer",
    )(x, *consts, pool0, conv0, lru0)


def _odd_kernel(x_ref, g_ref, wqkv_ref, wo_ref, sink_ref, ones_ref, kprev_ref, vprev_ref,
                o_ref, ktail_ref, vtail_ref,
                q_scr, k_buf, v_buf, att_scr, *, prefix_valid):
    j = pl.program_id(1)
    tq = x_ref.shape[1]
    nq = q_scr.shape[1]
    nkv = k_buf.shape[1]
    group = nq // nkv
    n_chunks = tq // CHUNK

    @pl.when(j == 0)
    def _():
        pad = jnp.zeros((KV_HIST - WINDOW, nkv), BF16)
        k_buf[0:KV_HIST - WINDOW, :] = pad
        v_buf[0:KV_HIST - WINDOW, :] = pad
        k_buf[KV_HIST - WINDOW:KV_HIST, :] = kprev_ref[0].astype(BF16)
        v_buf[KV_HIST - WINDOW:KV_HIST, :] = vprev_ref[0].astype(BF16)

    @pl.when(j > 0)
    def _():
        k_buf[0:KV_HIST, :] = k_buf[tq:tq + KV_HIST, :]
        v_buf[0:KV_HIST, :] = v_buf[tq:tq + KV_HIST, :]

    x = x_ref[0]
    h = _rms(x, g_ref[...]).astype(BF16)
    z = jnp.dot(h, wqkv_ref[...], preferred_element_type=F32)
    q_scr[...] = (z[:, :nq] * (HEAD_DIM ** -0.5)).astype(BF16)
    k_new = z[:, nq:nq + nkv]
    v_new = z[:, nq + nkv:]
    k_buf[KV_HIST:KV_HIST + tq, :] = k_new.astype(BF16)
    v_buf[KV_HIST:KV_HIST + tq, :] = v_new.astype(BF16)
    if tq >= WINDOW:
        ktail_ref[0] = k_new[tq - WINDOW:, :]
        vtail_ref[0] = v_new[tq - WINDOW:, :]
    else:
        ktail_ref[0] = jnp.concatenate([kprev_ref[0][tq:, :], k_new], axis=0)
        vtail_ref[0] = jnp.concatenate([vprev_ref[0][tq:, :], v_new], axis=0)

    lane = lax.broadcasted_iota(jnp.int32, (1, nkv), 1)
    head_masks = [(lane // HEAD_DIM) == kv for kv in range(N_KV_HEADS)]
    key_idx = lax.broadcasted_iota(jnp.int32, (1, KEY_SPAN), 1)
    pair_w = 2 * HEAD_DIM
    first_head = lax.broadcasted_iota(jnp.int32, (1, pair_w), 1) < HEAD_DIM

    for c in range(n_chunks):
        r0 = c * CHUNK
        qs = jnp.concatenate([q_scr[r0:r0 + CHUNK, gi * nkv:(gi + 1) * nkv] for gi in range(group)],
                             axis=0)
        kspan = k_buf[r0:r0 + KEY_SPAN, :]
        vspan = v_buf[r0:r0 + KEY_SPAN, :]
        zero = jnp.zeros_like(kspan)
        kbd = jnp.concatenate([jnp.where(m, kspan, zero) for m in head_masks], axis=0)
        s_all = lax.dot_general(qs, kbd, (((1,), (1,)), ((), ())), preferred_element_type=F32)
        key_pos = (j * tq - KV_HIST) + r0 + key_idx
        valid = key_idx >= (KEY_SPAN - WINDOW - CHUNK)
        if not prefix_valid:
            valid = valid & (key_pos >= 0)
        ps, mxs = [], []
        for kv in range(N_KV_HEADS):
            s = jnp.where(valid, s_all[:, kv * KEY_SPAN:(kv + 1) * KEY_SPAN], NEG_INF)
            mx = jnp.max(s, axis=-1, keepdims=True)
            ps.append(jnp.exp(s - mx).astype(BF16))
            mxs.append(mx)
        for pr in range(N_KV_HEADS // 2):
            lanes = slice(pr * pair_w, (pr + 1) * pair_w)
            vs = vspan[:, lanes]
            zero_v = jnp.zeros_like(vs)
            rhs = jnp.concatenate(
                [jnp.concatenate([jnp.where(first_head, vs, zero_v), jnp.where(first_head, zero_v, vs)],
                                 axis=0), ones_ref[...]], axis=1)
            o_all = jnp.dot(jnp.concatenate(ps[2 * pr:2 * pr + 2], axis=1), rhs,
                            preferred_element_type=F32)
            mx_pair = jnp.where(first_head, mxs[2 * pr], mxs[2 * pr + 1])
            den = o_all[:, pair_w:] + jnp.exp(sink_ref[:, lanes] - mx_pair)
            o = (o_all[:, :pair_w] * (1.0 / den)).astype(BF16)
            for gi in range(group):
                att_scr[r0:r0 + CHUNK, gi * nkv + pr * pair_w:gi * nkv + (pr + 1) * pair_w] = (
                    o[gi * CHUNK:(gi + 1) * CHUNK, :])

    o_ref[0] = x + jnp.dot(att_scr[...], wo_ref[...], preferred_element_type=F32)


def _odd_layer(x, g, w_qkv, w_o, sink_tab, k_prev, v_prev, prefix_valid, tile):
    B, L, D = x.shape
    nkv = k_prev.shape[2]
    nq = w_o.shape[0]
    head_ones = (jnp.arange(2 * KEY_SPAN)[:, None] // KEY_SPAN
                 == jnp.arange(2 * HEAD_DIM)[None, :] // HEAD_DIM).astype(BF16)
    tail = jax.ShapeDtypeStruct((B, WINDOW, nkv), F32)
    return pl.pallas_call(
        functools.partial(_odd_kernel, prefix_valid=prefix_valid),
        grid=(B, L // tile),
        in_specs=[
            pl.BlockSpec((1, tile, D), lambda b, j: (b, j, 0)),
            _const_spec((1, D)),
            _const_spec(w_qkv.shape),
            _const_spec(w_o.shape),
            _const_spec(sink_tab.shape),
            _const_spec(head_ones.shape),
            pl.BlockSpec((1, WINDOW, nkv), lambda b, j: (b, 0, 0)),
            pl.BlockSpec((1, WINDOW, nkv), lambda b, j: (b, 0, 0)),
        ],
        out_specs=[
            pl.BlockSpec((1, tile, D), lambda b, j: (b, j, 0)),
            pl.BlockSpec((1, WINDOW, nkv), lambda b, j: (b, 0, 0)),
            pl.BlockSpec((1, WINDOW, nkv), lambda b, j: (b, 0, 0)),
        ],
        out_shape=[jax.ShapeDtypeStruct((B, L, D), F32), tail, tail],
        scratch_shapes=[
            pltpu.VMEM((tile, nq), BF16),
            pltpu.VMEM((KV_HIST + tile, nkv), BF16),
            pltpu.VMEM((KV_HIST + tile, nkv), BF16),
            pltpu.VMEM((tile, nq), BF16),
        ],
        compiler_params=_cparams(),
        name="swa_mixer",
    )(x, g.reshape(1, D), w_qkv, w_o, sink_tab, head_ones, k_prev, v_prev)


def _block_diag(blocks):
    n, r, c = blocks.shape
    eye = jnp.eye(n, dtype=blocks.dtype)
    return (eye[:, None, :, None] * blocks[:, :, None, :]).reshape(n * r, n * c)


def _prep_even(p, e):
    pool_w = p["pool_w"][e]
    per_tile = MXU_DIM // pool_w.shape[1]
    pool_bd = jnp.stack([_block_diag(pool_w[i:i + per_tile])
                         for i in range(0, pool_w.shape[0], per_tile)])
    wa, wx = p["w_rg_a"][e], p["w_rg_x"][e]
    per_tile = MXU_DIM // wa.shape[1]
    w_ax = jnp.stack([jnp.concatenate([_block_diag(wa[i:i + per_tile]), _block_diag(wx[i:i + per_tile])],
                                      axis=1)
                      for i in range(0, wa.shape[0], per_tile)])
    return dict(
        w_in=p["w_in_even"][e].astype(BF16), pool_bd=pool_bd.astype(BF16), pool_scale=p["pool_scale"][e],
        conv_w=p["conv_w"][e], conv_b=p["conv_b"][e], w_ax=w_ax.astype(BF16), b_a=p["b_rg_a"][e],
        b_x=p["b_rg_x"][e], lam=p["rg_lambda"][e], w_out=p["w_out_even"][e].astype(BF16))


def _prep_odd(p, o):
    w_qkv = p["w_qkv_odd"][o]
    D = w_qkv.shape[0]
    w_o = p["w_o_odd"][o]
    nq = w_o.shape[0]
    group = nq // (N_KV_HEADS * HEAD_DIM)
    wq = w_qkv[:, :nq].reshape(D, N_KV_HEADS, group, HEAD_DIM).transpose(0, 2, 1, 3).reshape(D, nq)
    w_qkv = jnp.concatenate([wq, w_qkv[:, nq:]], axis=1)
    w_o = w_o.reshape(N_KV_HEADS, group, HEAD_DIM, -1).transpose(1, 0, 2, 3).reshape(nq, -1)
    sinks = p["attn_sinks"][o].reshape(N_KV_HEADS, group).T
    sink_tab = jnp.repeat(jnp.repeat(sinks, CHUNK, axis=0), HEAD_DIM, axis=1).astype(F32)
    return dict(w_qkv=w_qkv.astype(BF16), w_o=w_o.astype(BF16), sink_tab=sink_tab)


def _pad_rows(a, rows):
    return jnp.pad(a, ((0, 0), (rows - a.shape[1], 0), (0, 0)))


def _trunk(x, pos0, prefix_valid, pool_st, conv_st, lru_st, swa_k, swa_v, mem_k, mem_v, p, prep,
           tile, tile_mix):
    depth = p["norm_mix"].shape[0]
    B = x.shape[0]
    new_pool, new_conv, new_lru, new_k, new_v = [], [], [], [], []
    for layer in range(depth):
        if layer % 2 == 0:
            e = layer // 2
            x, pn, cn, ln = _even_layer(
                x, p["norm_mix"][layer], prep["even"][e],
                _pad_rows(pool_st[e], POOL_HIST), _pad_rows(conv_st[e], CONV_HIST),
                lru_st[e][:, None, :], pos0, min(x.shape[1], EVEN_SUBTILES * tile_mix),
                min(x.shape[1] // tile_mix, EVEN_SUBTILES))
            new_pool.append(pn[:, POOL_HIST - pool_st.shape[2]:])
            new_conv.append(cn[:, CONV_HIST - conv_st.shape[2]:])
            new_lru.append(ln[:, 0])
        else:
            o = layer // 2
            kp = swa_k[o].reshape(B, WINDOW, -1)
            vp = swa_v[o].reshape(B, WINDOW, -1)
            x, kn, vn = _odd_layer(x, p["norm_mix"][layer], prep["odd"][o]["w_qkv"], prep["odd"][o]["w_o"],
                                   prep["odd"][o]["sink_tab"], kp, vp, prefix_valid, tile_mix)
            new_k.append(kn.reshape(swa_k[o].shape))
            new_v.append(vn.reshape(swa_v[o].shape))
        L = x.shape[1]
        bt = max(1, tile // L)
        x = _cross(x, p["norm_cross"][layer], prep["w_mq"][layer], prep["w_mo"][layer],
                   mem_k, mem_v, layer, min(tile, L), min(bt, CROSS_MAX_BATCH_BLOCK, B))
        g_final = p["norm_final"] if layer == depth - 1 else None
        mlp_w = (p["norm_mlp"][layer], prep["w_up"][layer], prep["w_down"][layer], g_final)
        if L >= tile:
            x = _mlp(x, *mlp_w, tile)
        else:
            x = _mlp(x.reshape(1, B * L, -1), *mlp_w, min(tile, B * L)).reshape(B, L, -1)
    return (x, jnp.stack(new_pool), jnp.stack(new_conv), jnp.stack(new_lru),
            jnp.stack(new_k), jnp.stack(new_v))


def kernel(x_prompt, x_sample, state_pool, state_conv, state_lru, cache_swa_k, cache_swa_v,
           cache_mem_k, cache_mem_v, mem_prompt, norm_mix, norm_cross, norm_mem, norm_mlp, norm_final,
           w_in_even, conv_w, conv_b, w_rg_a, b_rg_a, w_rg_x, b_rg_x, rg_lambda, pool_w, pool_scale,
           w_out_even, w_qkv_odd, attn_sinks, w_o_odd, w_mq, w_mk, w_mv, w_mo, w_up, w_down):
    p = dict(norm_mix=norm_mix, norm_cross=norm_cross, norm_mlp=norm_mlp, norm_final=norm_final,
             w_in_even=w_in_even, conv_w=conv_w, conv_b=conv_b, w_rg_a=w_rg_a, b_rg_a=b_rg_a,
             w_rg_x=w_rg_x, b_rg_x=b_rg_x, rg_lambda=rg_lambda, pool_w=pool_w, pool_scale=pool_scale,
             w_out_even=w_out_even, w_qkv_odd=w_qkv_odd, attn_sinks=attn_sinks, w_o_odd=w_o_odd)
    depth = norm_mix.shape[0]
    n_even, n_odd = state_pool.shape[0], cache_swa_k.shape[0]
    prep = dict(
        even=[_prep_even(p, e) for e in range(n_even)],
        odd=[_prep_odd(p, o) for o in range(n_odd)],
        w_mq=w_mq.astype(BF16), w_mo=w_mo.astype(BF16),
        w_up=w_up.astype(BF16), w_down=w_down.astype(BF16))

    B, L, D = x_prompt.shape
    dt = x_prompt.dtype
    mem_k_p, mem_v_p, mem_k_packed, mem_v_packed = _mem_kv(
        mem_prompt, norm_mem, w_mk.astype(BF16), w_mv.astype(BF16), bb=2)

    zero_pool = jnp.zeros((n_even, B) + state_pool.shape[2:], dt)
    zero_conv = jnp.zeros((n_even, B) + state_conv.shape[2:], dt)
    zero_lru = jnp.zeros((n_even, B) + state_lru.shape[2:], dt)
    zero_kv = jnp.zeros((n_odd, B) + cache_swa_k.shape[2:], dt)
    y_prompt, pool_p, conv_p, lru_p, swa_k_p, swa_v_p = _trunk(
        x_prompt, 0, False, zero_pool, zero_conv, zero_lru, zero_kv, zero_kv, mem_k_packed, mem_v_packed,
        p, prep, tile=512, tile_mix=256)
    Ls = x_sample.shape[1]
    y_sample, pool_s, conv_s, lru_s, swa_k_s, swa_v_s = _trunk(
        x_sample, PAST_LEN, True, state_pool, state_conv, state_lru, cache_swa_k, cache_swa_v,
        cache_mem_k, cache_mem_v, p, prep, tile=512, tile_mix=Ls)
    return (y_prompt, y_sample, pool_p, conv_p, lru_p, swa_k_p, swa_v_p, mem_k_p, mem_v_p,
            pool_s, conv_s, lru_s, swa_k_s, swa_v_s)
```

```python
import functools

import jax
import jax.numpy as jnp
from jax import lax
from jax.experimental import pallas as pl
from jax.experimental.pallas import tpu as pltpu

F32 = jnp.float32
BF16 = jnp.bfloat16

EPS = 1e-6
NEG_INF = -1e30
CHUNK = 64
POOL_WINDOWS = (2, 4, 8, 16)
N_RNN_BLOCKS = 8
CONV_WIDTH = 4
RG_C = 8.0
HEAD_DIM = 64
N_KV_HEADS = 4
WINDOW = 128
N_MEM_HEADS = 4

LANES = 128
SUBLANES = 8
MXU_DIM = 256
VMEM_LIMIT_BYTES = 56 * 1024 * 1024

POOL_HIST = 16
CONV_HIST = 8
KEY_SPAN = WINDOW + 2 * CHUNK
KV_HIST = KEY_SPAN - CHUNK
CROSS_MAX_BATCH_BLOCK = 4
PAST_LEN = 2048
EVEN_SUBTILES = 2
EVEN_SUBTILE = 256
SWA_TILE = 512
CROSS_TILE = 512
MLP_TILE = 1024


def _cparams():
    return pltpu.CompilerParams(
        dimension_semantics=("arbitrary", "arbitrary"),
        vmem_limit_bytes=VMEM_LIMIT_BYTES)


def _const_spec(shape):
    zeros = (0,) * len(shape)
    return pl.BlockSpec(shape, lambda *_: zeros, pipeline_mode=pl.Buffered(1))


def _rms(x, g):
    ms = jnp.mean(x * x, axis=-1, keepdims=True)
    return x * lax.rsqrt(ms + EPS) * g


LOG2E = 1.4426950408889634
F32_TINY = 1.1754943508222875e-38
GELU_C0 = -2.0 * 0.7978845608028654 * LOG2E
GELU_C1 = GELU_C0 * 0.044715


def _sigmoid(v):
    return 1.0 / (1.0 + jnp.exp2(v * (-LOG2E)))


def _gelu_tanh(v):
    return v / (1.0 + jnp.exp2(v * (GELU_C1 * (v * v) + GELU_C0)))


def _scan_pitch(tile):
    p = max(-(-tile // SUBLANES), POOL_HIST)
    while p % 8 != 4:
        p += 1
    return p


def _mlp_kernel(x_ref, g_ref, wu_ref, wd_ref, *rest, final_norm, ff_chunk):
    if final_norm:
        gf_ref, o_ref, a_scr = rest
    else:
        o_ref, a_scr = rest
    x = x_ref[0]
    h = _rms(x, g_ref[...]).astype(BF16)
    d_ff = wu_ref.shape[1]
    for c in range(d_ff // ff_chunk):
        cols = slice(c * ff_chunk, (c + 1) * ff_chunk)
        u = jnp.dot(h, wu_ref[:, cols], preferred_element_type=F32)
        r = jnp.maximum(u, 0.0)
        a_scr[:, cols] = (r * r).astype(BF16)
    y = x + jnp.dot(a_scr[...], wd_ref[...], preferred_element_type=F32)
    if final_norm:
        y = _rms(y, gf_ref[...])
    o_ref[0] = y


def _mlp(x, g, w_up, w_down, g_final, tile):
    B, L, D = x.shape
    d_ff = w_up.shape[1]
    final_norm = g_final is not None
    in_specs = [
        pl.BlockSpec((1, tile, D), lambda b, j: (b, j, 0)),
        _const_spec((1, D)),
        _const_spec((D, d_ff)),
        _const_spec((d_ff, D)),
    ]
    args = [x, g.reshape(1, D), w_up, w_down]
    if final_norm:
        in_specs.append(_const_spec((1, D)))
        args.append(g_final.reshape(1, D))
    return pl.pallas_call(
        functools.partial(_mlp_kernel, final_norm=final_norm, ff_chunk=4 * MXU_DIM),
        grid=(B, L // tile),
        in_specs=in_specs,
        out_specs=pl.BlockSpec((1, tile, D), lambda b, j: (b, j, 0)),
        out_shape=jax.ShapeDtypeStruct((B, L, D), F32),
        scratch_shapes=[pltpu.VMEM((tile, d_ff), BF16)],
        compiler_params=_cparams(),
        name="mlp",
    )(*args)


def _cross_kernel(x_ref, g_ref, wq_ref, wo_ref, *rest, packed):
    bt, tile, D = x_ref.shape
    dh = D // N_MEM_HEADS
    if packed:
        k_src, v_src, o_ref, att_scr = rest
    else:
        n_lb = dh // LANES
        mk_refs, mv_refs = rest[:n_lb], rest[n_lb:2 * n_lb]
        o_ref, att_scr, k_src, v_src = rest[2 * n_lb:]

        @pl.when(pl.program_id(1) == 0)
        def _():
            n_mem = k_src.shape[2]
            for src_refs, dst in ((mk_refs, k_src), (mv_refs, v_src)):
                for bi in range(bt):
                    for lb in range(n_lb):
                        rows = src_refs[lb].at[0, bi].reshape(n_mem * N_MEM_HEADS, LANES)
                        for hd in range(N_MEM_HEADS):
                            lanes = slice(hd * dh + lb * LANES, hd * dh + (lb + 1) * LANES)
                            dst[0, bi, :, lanes] = rows[pl.ds(hd, n_mem, stride=N_MEM_HEADS), :].astype(BF16)

    x = x_ref[...].reshape(bt * tile, D)
    h = _rms(x, g_ref[...]).astype(BF16)
    q = jnp.dot(h, wq_ref[...], preferred_element_type=F32)
    q = (q * (dh ** -0.5)).astype(BF16)
    def scores(bi, hd):
        return lax.dot_general(q[bi * tile:(bi + 1) * tile, hd * dh:(hd + 1) * dh],
                               k_src[0, bi, :, hd * dh:(hd + 1) * dh], (((1,), (1,)), ((), ())),
                               preferred_element_type=F32)

    def attend(bi, hd, s):
        mx = jnp.max(s, axis=-1, keepdims=True)
        p = jnp.exp(s - mx)
        den = jnp.sum(p, axis=-1, keepdims=True)
        o = jnp.dot(p.astype(BF16), v_src[0, bi, :, hd * dh:(hd + 1) * dh], preferred_element_type=F32)
        att_scr[bi * tile:(bi + 1) * tile, hd * dh:(hd + 1) * dh] = (o * (1.0 / den)).astype(BF16)

    units = [(bi, hd) for bi in range(bt) for hd in range(N_MEM_HEADS)]
    s_cur = scores(*units[0])
    for n, unit in enumerate(units):
        s_next = scores(*units[n + 1]) if n + 1 < len(units) else None
        attend(*unit, s_cur)
        s_cur = s_next
    y = x + jnp.dot(att_scr[...], wo_ref[...], preferred_element_type=F32)
    o_ref[...] = y.reshape(bt, tile, D)


def _cross(x, g, w_q, w_o, mem_k, mem_v, layer, tile, bt):
    B, L, D = x.shape
    n_mem = mem_k.shape[2]
    packed = mem_k.ndim == 4
    scratch = [pltpu.VMEM((bt * tile, D), BF16)]
    if packed:
        mem_specs = [pl.BlockSpec((1, bt, n_mem, D), lambda b, j: (layer, b, 0, 0))]
    else:
        n_heads, dh = mem_k.shape[3:]
        mem_specs = [pl.BlockSpec((1, bt, n_mem, n_heads, LANES),
                                  functools.partial(lambda b, j, lb: (layer, b, 0, 0, lb), lb=lb))
                     for lb in range(dh // LANES)]
        scratch += [pltpu.VMEM((1, bt, n_mem, D), BF16)] * 2
    n_blk = len(mem_specs)
    return pl.pallas_call(
        functools.partial(_cross_kernel, packed=packed),
        grid=(B // bt, L // tile),
        in_specs=[
            pl.BlockSpec((bt, tile, D), lambda b, j: (b, j, 0)),
            _const_spec((1, D)),
            _const_spec((D, D)),
            _const_spec((D, D)),
        ] + mem_specs + mem_specs,
        out_specs=pl.BlockSpec((bt, tile, D), lambda b, j: (b, j, 0)),
        out_shape=jax.ShapeDtypeStruct((B, L, D), F32),
        scratch_shapes=scratch,
        compiler_params=_cparams(),
        name="cross_attn",
    )(x, g.reshape(1, D), w_q, w_o, *([mem_k] * n_blk), *([mem_v] * n_blk))


def _memkv_kernel(m_ref, g_ref, wk_ref, wv_ref, k_ref, v_ref, kp_ref, vp_ref):
    bb, n_mem, D = m_ref.shape
    dh = k_ref.shape[4]
    mn = _rms(m_ref[...].reshape(bb * n_mem, D), g_ref[0]).astype(BF16)
    k = jnp.dot(mn, wk_ref[0], preferred_element_type=F32)
    v = jnp.dot(mn, wv_ref[0], preferred_element_type=F32)
    kp_ref[0] = k.astype(BF16).reshape(bb, n_mem, D)
    vp_ref[0] = v.astype(BF16).reshape(bb, n_mem, D)
    for bi in range(bb):
        rows = slice(bi * n_mem, (bi + 1) * n_mem)
        for hd in range(N_MEM_HEADS):
            k_ref[0, bi, :, hd, :] = k[rows, hd * dh:(hd + 1) * dh]
            v_ref[0, bi, :, hd, :] = v[rows, hd * dh:(hd + 1) * dh]


def _mem_kv(mem, g, w_k, w_v, bb):
    depth, D = g.shape
    B, n_mem, _ = mem.shape
    dh = D // N_MEM_HEADS
    out = jax.ShapeDtypeStruct((depth, B, n_mem, N_MEM_HEADS, dh), F32)
    out_spec = pl.BlockSpec((1, bb, n_mem, N_MEM_HEADS, dh), lambda l, j: (l, j, 0, 0, 0))
    packed = jax.ShapeDtypeStruct((depth, B, n_mem, D), BF16)
    packed_spec = pl.BlockSpec((1, bb, n_mem, D), lambda l, j: (l, j, 0, 0))
    return pl.pallas_call(
        _memkv_kernel,
        grid=(depth, B // bb),
        in_specs=[
            pl.BlockSpec((bb, n_mem, D), lambda l, j: (j, 0, 0)),
            pl.BlockSpec((1, 1, D), lambda l, j: (l, 0, 0)),
            pl.BlockSpec((1, D, D), lambda l, j: (l, 0, 0)),
            pl.BlockSpec((1, D, D), lambda l, j: (l, 0, 0)),
        ],
        out_specs=[out_spec, out_spec, packed_spec, packed_spec],
        out_shape=[out, out, packed, packed],
        compiler_params=_cparams(),
        name="mem_kv",
    )(mem, g.reshape(depth, 1, D), w_k, w_v)


def _even_kernel(x_ref, g_ref, win_ref, pbd_ref, pscale_ref, cw_ref, cb_ref, wax_ref, ba_ref, bx_ref,
                 lam_ref, wout_ref, pool0_ref, conv0_ref, lru0_ref,
                 o_ref, pool_o, conv_o, lru_o,
                 hcar, *sub_scratch, pos0, pitch):
    j = pl.program_id(1)
    n_sub = len(sub_scratch) // 2
    zs_list, ys_list = sub_scratch[:n_sub], sub_scratch[n_sub:]
    n_in = zs_list[0].shape[0]
    tl = x_ref.shape[1] // n_sub
    dp = pbd_ref.shape[0] * pbd_ref.shape[1]
    dr = cw_ref.shape[1]
    n_pool = dp // LANES
    n_rnn = dr // LANES
    rows_p = SUBLANES * pitch
    half = MXU_DIM
    per_half = half // LANES

    @pl.when(j == 0)
    def _():
        zs_tail = zs_list[n_sub - 1]
        for k in range(n_pool):
            zs_tail[k, tl - POOL_HIST:tl, :] = pool0_ref[0, :, k * LANES:(k + 1) * LANES]
        for k in range(n_rnn):
            zs_tail[n_pool + k, tl - CONV_HIST:tl, :] = conv0_ref[0, :, k * LANES:(k + 1) * LANES]
        hcar[...] = lru0_ref[0]
        for zs in zs_list:
            zs[:, tl:, :] = jnp.zeros((n_in, rows_p - tl, LANES), F32)

    subs = [_even_subtile(s, n_sub, tl, x_ref, g_ref, win_ref, pbd_ref, pscale_ref, cw_ref, cb_ref, wax_ref,
                          ba_ref, bx_ref, lam_ref, wout_ref, o_ref, pool_o, conv_o, lru_o,
                          zs_list[(s - 1) % n_sub], zs_list[s], ys_list[s], hcar,
                          pos0 + (j * n_sub + s) * tl, pitch) for s in range(n_sub)]
    next(subs[0])
    for s in range(n_sub):
        next(subs[s])
        if s + 1 < n_sub:
            next(subs[s + 1])
        for _ in subs[s]:
            pass


def _even_subtile(s, n_sub, tl, x_ref, g_ref, win_ref, pbd_ref, pscale_ref, cw_ref, cb_ref, wax_ref,
                  ba_ref, bx_ref, lam_ref, wout_ref, o_ref, pool_o, conv_o, lru_o,
                  zs_prev, zs, ys, hcar, pos_tile, pitch):
    dp = pbd_ref.shape[0] * pbd_ref.shape[1]
    dr = cw_ref.shape[1]
    n_pool = dp // LANES
    n_rnn = dr // LANES
    n_in = zs.shape[0]
    half = MXU_DIM
    per_half = half // LANES
    rows = slice(s * tl, (s + 1) * tl)
    is_last = s == n_sub - 1

    hist_pool = [zs_prev[k, tl - POOL_HIST:tl, :] for k in range(n_pool)]
    hist_x = [zs_prev[n_pool + k, tl - CONV_HIST:tl, :] for k in range(n_rnn)]

    x = x_ref[0, rows, :]
    h = _rms(x, g_ref[...]).astype(BF16)
    z = jnp.dot(h, win_ref[...], preferred_element_type=F32)
    for k in range(n_in):
        zs[k, 0:tl, :] = z[:, k * LANES:(k + 1) * LANES]
    if is_last:
        pool_o[0] = z[tl - POOL_HIST:, :dp]
        conv_o[0] = z[tl - CONV_HIST:, dp:dp + dr]
    yield

    def gather(k):
        return [zs.at[k][pl.ds(i, SUBLANES, stride=pitch), :] for i in range(pitch)]

    def stack(vregs):
        return jnp.concatenate(vregs, axis=0)

    def unstack(arr, lane_slab):
        return [arr[i * SUBLANES:(i + 1) * SUBLANES, lane_slab * LANES:(lane_slab + 1) * LANES]
                for i in range(pitch)]

    sub = lax.broadcasted_iota(jnp.int32, (SUBLANES, LANES), 0)
    first_chunk = sub == 0

    def with_history(cur, hist, depth):
        n_hist = hist.shape[0]
        wraps = []
        for m in range(depth, 0, -1):
            prev_end = pltpu.roll(cur[pitch - m], 1, axis=0)
            wraps.append(jnp.where(first_chunk, hist[n_hist - m:n_hist - m + 1, :], prev_end))
        return wraps + cur

    pos_first = pos_tile + sub * pitch
    d_slabs = []
    for gi, win in enumerate(POOL_WINDOWS):
        u = gather(gi)
        level = with_history(u, hist_pool[gi], win - 1)
        span = 1
        while span < win:
            level = [level[n] + level[n - span] for n in range(span, len(level))]
            span *= 2
        d = []
        for i in range(pitch):
            if i >= win - 1:
                inv_cnt = 1.0 / win
            else:
                inv_cnt = 1.0 / jnp.minimum(pos_first + (i + 1), win).astype(F32)
            d.append(level[i] * inv_cnt - u[i])
        d_slabs.append(stack(d))
    y_pool = []
    for hf in range(n_pool // per_half):
        cols = slice(hf * half, (hf + 1) * half)
        dd = jnp.concatenate(d_slabs[hf * per_half:(hf + 1) * per_half], axis=1).astype(BF16)
        yp = jnp.dot(dd, pbd_ref[hf], preferred_element_type=F32)
        y_pool.append((yp * pscale_ref[:, cols]).astype(BF16))

    xc_slabs = []
    for k in range(n_rnn):
        lanes = slice(k * LANES, (k + 1) * LANES)
        ext = with_history(gather(n_pool + k), hist_x[k], CONV_WIDTH - 1)
        taps = [jnp.broadcast_to(cw_ref[t:t + 1, lanes], (SUBLANES, LANES)) for t in range(CONV_WIDTH)]
        bias = jnp.broadcast_to(cb_ref[:, lanes], (SUBLANES, LANES))
        xc = []
        for i in range(pitch):
            acc = bias
            for t in range(CONV_WIDTH):
                acc = acc + ext[i + t] * taps[t]
            xc.append(acc)
        xc_slabs.append(stack(xc))

    c_lam = -RG_C * jnp.log1p(jnp.exp(-lam_ref[...]))
    t_last = tl - 1
    h_slabs, h_last = [], []
    xc_halves = [jnp.concatenate(xc_slabs[hf * per_half:(hf + 1) * per_half], axis=1)
                 for hf in range(n_rnn // per_half)]
    gate_halves = [jnp.dot(xc.astype(BF16), wax_ref[hf], preferred_element_type=F32)
                   for hf, xc in enumerate(xc_halves)]
    yield
    for hf in range(n_rnn // per_half):
        cols = slice(hf * half, (hf + 1) * half)
        xc, gates = xc_halves[hf], gate_halves[hf]
        r_gate = _sigmoid(gates[:, :half] + ba_ref[:, cols])
        i_gate = _sigmoid(gates[:, half:] + bx_ref[:, cols])
        a = jnp.exp2(r_gate * (c_lam[:, cols] * LOG2E))
        one_minus_a2 = jnp.tanh(r_gate * (-c_lam[:, cols])) * (a * a + 1.0)
        mult = one_minus_a2 * lax.rsqrt(jnp.maximum(one_minus_a2, F32_TINY))
        bb = mult * i_gate * xc
        for sl in range(per_half):
            k = hf * per_half + sl
            lanes = slice(k * LANES, (k + 1) * LANES)
            a_v, b_v = unstack(a, sl), unstack(bb, sl)
            hv = jnp.zeros((SUBLANES, LANES), F32)
            pv = jnp.ones((SUBLANES, LANES), F32)
            h_loc, p_loc = [], []
            for i in range(pitch):
                hv = a_v[i] * hv + b_v[i]
                pv = a_v[i] * pv
                h_loc.append(hv)
                p_loc.append(pv)
            carry = [hcar[:, lanes]]
            for r in range(1, SUBLANES):
                carry.append(hv[r - 1:r, :] + pv[r - 1:r, :] * carry[-1])
            cv = jnp.concatenate(carry, axis=0)
            h_fix = [h_loc[i] + p_loc[i] * cv for i in range(pitch)]
            last = h_fix[t_last % pitch][t_last // pitch:t_last // pitch + 1, :]
            hcar[:, lanes] = last
            h_last.append(last)
            h_slabs.append(stack(h_fix))
    if is_last:
        lru_o[0] = jnp.concatenate(h_last, axis=1)

    gate = jnp.concatenate([stack(gather(n_pool + n_rnn + k)) for k in range(n_rnn)], axis=1)
    y_rnn = (jnp.concatenate(h_slabs, axis=1) * _gelu_tanh(gate)).astype(BF16)
    cat = jnp.concatenate(y_pool + [y_rnn], axis=1)
    y = jnp.dot(cat, wout_ref[...], preferred_element_type=F32)
    for k in range(ys.shape[0]):
        for i in range(pitch):
            ys.at[k][pl.ds(i, SUBLANES, stride=pitch), :] = (
                y[i * SUBLANES:(i + 1) * SUBLANES, k * LANES:(k + 1) * LANES])
    o_ref[0, rows, :] = x + jnp.concatenate([ys[k, 0:tl, :] for k in range(ys.shape[0])], axis=1)


def _even_layer(x, g, wts, pool0, conv0, lru0, pos0, tile, n_sub):
    B, L, D = x.shape
    dp = pool0.shape[2]
    dr = conv0.shape[2]
    pitch = _scan_pitch(tile // n_sub)
    row = lambda v: v.reshape(1, -1)
    consts = [row(g), wts["w_in"], wts["pool_bd"], row(wts["pool_scale"]), wts["conv_w"],
              row(wts["conv_b"]), wts["w_ax"], row(wts["b_a"]), row(wts["b_x"]),
              row(wts["lam"]), wts["w_out"]]
    in_specs = [pl.BlockSpec((1, tile, D), lambda b, j: (b, j, 0))]
    in_specs += [_const_spec(c.shape) for c in consts]
    in_specs += [
        pl.BlockSpec((1, POOL_HIST, dp), lambda b, j: (b, 0, 0)),
        pl.BlockSpec((1, CONV_HIST, dr), lambda b, j: (b, 0, 0)),
        pl.BlockSpec((1, 1, dr), lambda b, j: (b, 0, 0)),
    ]
    d_in = wts["w_in"].shape[1]
    return pl.pallas_call(
        functools.partial(_even_kernel, pos0=pos0, pitch=pitch),
        grid=(B, L // tile),
        in_specs=in_specs,
        out_specs=[
            pl.BlockSpec((1, tile, D), lambda b, j: (b, j, 0)),
            pl.BlockSpec((1, POOL_HIST, dp), lambda b, j: (b, 0, 0)),
            pl.BlockSpec((1, CONV_HIST, dr), lambda b, j: (b, 0, 0)),
            pl.BlockSpec((1, 1, dr), lambda b, j: (b, 0, 0)),
        ],
        out_shape=[
            jax.ShapeDtypeStruct((B, L, D), F32),
            jax.ShapeDtypeStruct((B, POOL_HIST, dp), F32),
            jax.ShapeDtypeStruct((B, CONV_HIST, dr), F32),
            jax.ShapeDtypeStruct((B, 1, dr), F32),
        ],
        scratch_shapes=(
            [pltpu.VMEM((1, dr), F32)]
            + [pltpu.VMEM((d_in // LANES, SUBLANES * pitch, LANES), F32)] * n_sub
            + [pltpu.VMEM((D // LANES, SUBLANES * pitch, LANES), F32)] * n_sub),
        compiler_params=_cparams(),
        name="even_mixer",
    )(x, *consts, pool0, conv0, lru0)


def _odd_kernel(x_ref, g_ref, wqkv_ref, wo_ref, sink_ref, ones_ref, kprev_ref, vprev_ref,
                o_ref, ktail_ref, vtail_ref,
                q_scr, k_buf, v_buf, att_scr, *, prefix_valid):
    j = pl.program_id(1)
    tq = x_ref.shape[1]
    nq = q_scr.shape[1]
    nkv = k_buf.shape[1]
    group = nq // nkv
    n_chunks = tq // CHUNK

    @pl.when(j == 0)
    def _():
        pad = jnp.zeros((KV_HIST - WINDOW, nkv), BF16)
        k_buf[0:KV_HIST - WINDOW, :] = pad
        v_buf[0:KV_HIST - WINDOW, :] = pad
        k_buf[KV_HIST - WINDOW:KV_HIST, :] = kprev_ref[0].astype(BF16)
        v_buf[KV_HIST - WINDOW:KV_HIST, :] = vprev_ref[0].astype(BF16)

    @pl.when(j > 0)
    def _():
        k_buf[0:KV_HIST, :] = k_buf[tq:tq + KV_HIST, :]
        v_buf[0:KV_HIST, :] = v_buf[tq:tq + KV_HIST, :]

    x = x_ref[0]
    h = _rms(x, g_ref[...]).astype(BF16)
    z = jnp.dot(h, wqkv_ref[...], preferred_element_type=F32)
    q_scr[...] = (z[:, :nq] * (HEAD_DIM ** -0.5)).astype(BF16)
    k_new = z[:, nq:nq + nkv]
    v_new = z[:, nq + nkv:]
    k_buf[KV_HIST:KV_HIST + tq, :] = k_new.astype(BF16)
    v_buf[KV_HIST:KV_HIST + tq, :] = v_new.astype(BF16)
    if tq >= WINDOW:
        ktail_ref[0] = k_new[tq - WINDOW:, :]
        vtail_ref[0] = v_new[tq - WINDOW:, :]
    else:
        ktail_ref[0] = jnp.concatenate([kprev_ref[0][tq:, :], k_new], axis=0)
        vtail_ref[0] = jnp.concatenate([vprev_ref[0][tq:, :], v_new], axis=0)

    lane = lax.broadcasted_iota(jnp.int32, (1, nkv), 1)
    head_masks = [(lane // HEAD_DIM) == kv for kv in range(N_KV_HEADS)]
    key_idx = lax.broadcasted_iota(jnp.int32, (1, KEY_SPAN), 1)
    pair_w = 2 * HEAD_DIM
    first_head = lax.broadcasted_iota(jnp.int32, (1, pair_w), 1) < HEAD_DIM

    def chunk_scores(c):
        r0 = c * CHUNK
        qs = jnp.concatenate([q_scr[r0:r0 + CHUNK, gi * nkv:(gi + 1) * nkv] for gi in range(group)],
                             axis=0)
        kspan = k_buf[r0:r0 + KEY_SPAN, :]
        zero = jnp.zeros_like(kspan)
        kbd = jnp.concatenate([jnp.where(m, kspan, zero) for m in head_masks], axis=0)
        return lax.dot_general(qs, kbd, (((1,), (1,)), ((), ())), preferred_element_type=F32)

    def chunk_attend(c, s_all):
        r0 = c * CHUNK
        vspan = v_buf[r0:r0 + KEY_SPAN, :]
        key_pos = (j * tq - KV_HIST) + r0 + key_idx
        valid = key_idx >= (KEY_SPAN - WINDOW - CHUNK)
        if not prefix_valid:
            valid = valid & (key_pos >= 0)
        ps, mxs = [], []
        for kv in range(N_KV_HEADS):
            s = jnp.where(valid, s_all[:, kv * KEY_SPAN:(kv + 1) * KEY_SPAN], NEG_INF)
            mx = jnp.max(s, axis=-1, keepdims=True)
            ps.append(jnp.exp(s - mx).astype(BF16))
            mxs.append(mx)
        for pr in range(N_KV_HEADS // 2):
            lanes = slice(pr * pair_w, (pr + 1) * pair_w)
            vs = vspan[:, lanes]
            zero_v = jnp.zeros_like(vs)
            rhs = jnp.concatenate(
                [jnp.concatenate([jnp.where(first_head, vs, zero_v), jnp.where(first_head, zero_v, vs)],
                                 axis=0), ones_ref[...]], axis=1)
            o_all = jnp.dot(jnp.concatenate(ps[2 * pr:2 * pr + 2], axis=1), rhs,
                            preferred_element_type=F32)
            mx_pair = jnp.where(first_head, mxs[2 * pr], mxs[2 * pr + 1])
            den = o_all[:, pair_w:] + jnp.exp(sink_ref[:, lanes] - mx_pair)
            o = (o_all[:, :pair_w] * (1.0 / den)).astype(BF16)
            for gi in range(group):
                att_scr[r0:r0 + CHUNK, gi * nkv + pr * pair_w:gi * nkv + (pr + 1) * pair_w] = (
                    o[gi * CHUNK:(gi + 1) * CHUNK, :])

    s_cur = chunk_scores(0)
    for c in range(n_chunks):
        s_next = chunk_scores(c + 1) if c + 1 < n_chunks else None
        chunk_attend(c, s_cur)
        s_cur = s_next

    o_ref[0] = x + jnp.dot(att_scr[...], wo_ref[...], preferred_element_type=F32)


def _odd_layer(x, g, w_qkv, w_o, sink_tab, k_prev, v_prev, prefix_valid, tile):
    B, L, D = x.shape
    nkv = k_prev.shape[2]
    nq = w_o.shape[0]
    head_ones = (jnp.arange(2 * KEY_SPAN)[:, None] // KEY_SPAN
                 == jnp.arange(2 * HEAD_DIM)[None, :] // HEAD_DIM).astype(BF16)
    tail = jax.ShapeDtypeStruct((B, WINDOW, nkv), F32)
    return pl.pallas_call(
        functools.partial(_odd_kernel, prefix_valid=prefix_valid),
        grid=(B, L // tile),
        in_specs=[
            pl.BlockSpec((1, tile, D), lambda b, j: (b, j, 0)),
            _const_spec((1, D)),
            _const_spec(w_qkv.shape),
            _const_spec(w_o.shape),
            _const_spec(sink_tab.shape),
            _const_spec(head_ones.shape),
            pl.BlockSpec((1, WINDOW, nkv), lambda b, j: (b, 0, 0)),
            pl.BlockSpec((1, WINDOW, nkv), lambda b, j: (b, 0, 0)),
        ],
        out_specs=[
            pl.BlockSpec((1, tile, D), lambda b, j: (b, j, 0)),
            pl.BlockSpec((1, WINDOW, nkv), lambda b, j: (b, 0, 0)),
            pl.BlockSpec((1, WINDOW, nkv), lambda b, j: (b, 0, 0)),
        ],
        out_shape=[jax.ShapeDtypeStruct((B, L, D), F32), tail, tail],
        scratch_shapes=[
            pltpu.VMEM((tile, nq), BF16),
            pltpu.VMEM((KV_HIST + tile, nkv), BF16),
            pltpu.VMEM((KV_HIST + tile, nkv), BF16),
            pltpu.VMEM((tile, nq), BF16),
        ],
        compiler_params=_cparams(),
        name="swa_mixer",
    )(x, g.reshape(1, D), w_qkv, w_o, sink_tab, head_ones, k_prev, v_prev)


def _block_diag(blocks):
    n, r, c = blocks.shape
    eye = jnp.eye(n, dtype=blocks.dtype)
    return (eye[:, None, :, None] * blocks[:, :, None, :]).reshape(n * r, n * c)


def _prep_even(p, e):
    pool_w = p["pool_w"][e]
    per_tile = MXU_DIM // pool_w.shape[1]
    pool_bd = jnp.stack([_block_diag(pool_w[i:i + per_tile])
                         for i in range(0, pool_w.shape[0], per_tile)])
    wa, wx = p["w_rg_a"][e], p["w_rg_x"][e]
    per_tile = MXU_DIM // wa.shape[1]
    w_ax = jnp.stack([jnp.concatenate([_block_diag(wa[i:i + per_tile]), _block_diag(wx[i:i + per_tile])],
                                      axis=1)
                      for i in range(0, wa.shape[0], per_tile)])
    return dict(
        w_in=p["w_in_even"][e].astype(BF16), pool_bd=pool_bd.astype(BF16), pool_scale=p["pool_scale"][e],
        conv_w=p["conv_w"][e], conv_b=p["conv_b"][e], w_ax=w_ax.astype(BF16), b_a=p["b_rg_a"][e],
        b_x=p["b_rg_x"][e], lam=p["rg_lambda"][e], w_out=p["w_out_even"][e].astype(BF16))


def _prep_odd(p, o):
    w_qkv = p["w_qkv_odd"][o]
    D = w_qkv.shape[0]
    w_o = p["w_o_odd"][o]
    nq = w_o.shape[0]
    group = nq // (N_KV_HEADS * HEAD_DIM)
    wq = w_qkv[:, :nq].reshape(D, N_KV_HEADS, group, HEAD_DIM).transpose(0, 2, 1, 3).reshape(D, nq)
    w_qkv = jnp.concatenate([wq, w_qkv[:, nq:]], axis=1)
    w_o = w_o.reshape(N_KV_HEADS, group, HEAD_DIM, -1).transpose(1, 0, 2, 3).reshape(nq, -1)
    sinks = p["attn_sinks"][o].reshape(N_KV_HEADS, group).T
    sink_tab = jnp.repeat(jnp.repeat(sinks, CHUNK, axis=0), HEAD_DIM, axis=1).astype(F32)
    return dict(w_qkv=w_qkv.astype(BF16), w_o=w_o.astype(BF16), sink_tab=sink_tab)


def _pad_rows(a, rows):
    return jnp.pad(a, ((0, 0), (rows - a.shape[1], 0), (0, 0)))


def _trunk(x, pos0, prefix_valid, pool_st, conv_st, lru_st, swa_k, swa_v, mem_k, mem_v, p, prep):
    depth = p["norm_mix"].shape[0]
    B, L = x.shape[:2]
    even_sub = min(L, EVEN_SUBTILE)
    even_n_sub = min(L // even_sub, EVEN_SUBTILES)
    new_pool, new_conv, new_lru, new_k, new_v = [], [], [], [], []
    for layer in range(depth):
        if layer % 2 == 0:
            e = layer // 2
            x, pn, cn, ln = _even_layer(
                x, p["norm_mix"][layer], prep["even"][e],
                _pad_rows(pool_st[e], POOL_HIST), _pad_rows(conv_st[e], CONV_HIST),
                lru_st[e][:, None, :], pos0, even_sub * even_n_sub, even_n_sub)
            new_pool.append(pn[:, POOL_HIST - pool_st.shape[2]:])
            new_conv.append(cn[:, CONV_HIST - conv_st.shape[2]:])
            new_lru.append(ln[:, 0])
        else:
            o = layer // 2
            kp = swa_k[o].reshape(B, WINDOW, -1)
            vp = swa_v[o].reshape(B, WINDOW, -1)
            x, kn, vn = _odd_layer(x, p["norm_mix"][layer], prep["odd"][o]["w_qkv"], prep["odd"][o]["w_o"],
                                   prep["odd"][o]["sink_tab"], kp, vp, prefix_valid, min(L, SWA_TILE))
            new_k.append(kn.reshape(swa_k[o].shape))
            new_v.append(vn.reshape(swa_v[o].shape))
        bt = max(1, CROSS_TILE // L)
        x = _cross(x, p["norm_cross"][layer], prep["w_mq"][layer], prep["w_mo"][layer],
                   mem_k, mem_v, layer, min(CROSS_TILE, L), min(bt, CROSS_MAX_BATCH_BLOCK, B))
        g_final = p["norm_final"] if layer == depth - 1 else None
        mlp_w = (p["norm_mlp"][layer], prep["w_up"][layer], prep["w_down"][layer], g_final)
        if L >= MLP_TILE:
            x = _mlp(x, *mlp_w, MLP_TILE)
        else:
            x = _mlp(x.reshape(1, B * L, -1), *mlp_w, min(MLP_TILE, B * L)).reshape(B, L, -1)
    return (x, jnp.stack(new_pool), jnp.stack(new_conv), jnp.stack(new_lru),
            jnp.stack(new_k), jnp.stack(new_v))


def kernel(x_prompt, x_sample, state_pool, state_conv, state_lru, cache_swa_k, cache_swa_v,
           cache_mem_k, cache_mem_v, mem_prompt, norm_mix, norm_cross, norm_mem, norm_mlp, norm_final,
           w_in_even, conv_w, conv_b, w_rg_a, b_rg_a, w_rg_x, b_rg_x, rg_lambda, pool_w, pool_scale,
           w_out_even, w_qkv_odd, attn_sinks, w_o_odd, w_mq, w_mk, w_mv, w_mo, w_up, w_down):
    p = dict(norm_mix=norm_mix, norm_cross=norm_cross, norm_mlp=norm_mlp, norm_final=norm_final,
             w_in_even=w_in_even, conv_w=conv_w, conv_b=conv_b, w_rg_a=w_rg_a, b_rg_a=b_rg_a,
             w_rg_x=w_rg_x, b_rg_x=b_rg_x, rg_lambda=rg_lambda, pool_w=pool_w, pool_scale=pool_scale,
             w_out_even=w_out_even, w_qkv_odd=w_qkv_odd, attn_sinks=attn_sinks, w_o_odd=w_o_odd)
    depth = norm_mix.shape[0]
    n_even, n_odd = state_pool.shape[0], cache_swa_k.shape[0]
    prep = dict(
        even=[_prep_even(p, e) for e in range(n_even)],
        odd=[_prep_odd(p, o) for o in range(n_odd)],
        w_mq=w_mq.astype(BF16), w_mo=w_mo.astype(BF16),
        w_up=w_up.astype(BF16), w_down=w_down.astype(BF16))

    B, L, D = x_prompt.shape
    dt = x_prompt.dtype
    mem_k_p, mem_v_p, mem_k_packed, mem_v_packed = _mem_kv(
        mem_prompt, norm_mem, w_mk.astype(BF16), w_mv.astype(BF16), bb=2)

    zero_pool = jnp.zeros((n_even, B) + state_pool.shape[2:], dt)
    zero_conv = jnp.zeros((n_even, B) + state_conv.shape[2:], dt)
    zero_lru = jnp.zeros((n_even, B) + state_lru.shape[2:], dt)
    zero_kv = jnp.zeros((n_odd, B) + cache_swa_k.shape[2:], dt)
    y_prompt, pool_p, conv_p, lru_p, swa_k_p, swa_v_p = _trunk(
        x_prompt, 0, False, zero_pool, zero_conv, zero_lru, zero_kv, zero_kv, mem_k_packed, mem_v_packed,
        p, prep)
    y_sample, pool_s, conv_s, lru_s, swa_k_s, swa_v_s = _trunk(
        x_sample, PAST_LEN, True, state_pool, state_conv, state_lru, cache_swa_k, cache_swa_v,
        cache_mem_k, cache_mem_v, p, prep)
    return (y_prompt, y_sample, pool_p, conv_p, lru_p, swa_k_p, swa_v_p, mem_k_p, mem_v_p,
            pool_s, conv_s, lru_s, swa_k_s, swa_v_s)
```

```python
import functools

import jax
import jax.numpy as jnp
from jax import lax
from jax.experimental import pallas as pl
from jax.experimental.pallas import tpu as pltpu

F32 = jnp.float32
BF16 = jnp.bfloat16

EPS = 1e-6
NEG_INF = -1e30
CHUNK = 64
POOL_WINDOWS = (2, 4, 8, 16)
N_RNN_BLOCKS = 8
CONV_WIDTH = 4
RG_C = 8.0
HEAD_DIM = 64
N_KV_HEADS = 4
WINDOW = 128
N_MEM_HEADS = 4

LANES = 128
SUBLANES = 8
MXU_DIM = 256
VMEM_LIMIT_BYTES = 56 * 1024 * 1024

POOL_HIST = 16
CONV_HIST = 8
KEY_SPAN = WINDOW + 2 * CHUNK
KV_HIST = KEY_SPAN - CHUNK
CROSS_MAX_BATCH_BLOCK = 4
PAST_LEN = 2048
EVEN_SUBTILES = 4
EVEN_SUBTILE = 256
SWA_TILE = 512
CROSS_TILE = 1024
MLP_TILE = 1024


def _cparams():
    return pltpu.CompilerParams(
        dimension_semantics=("arbitrary", "arbitrary"),
        vmem_limit_bytes=VMEM_LIMIT_BYTES)


def _const_spec(shape):
    zeros = (0,) * len(shape)
    return pl.BlockSpec(shape, lambda *_: zeros, pipeline_mode=pl.Buffered(1))


def _rms(x, g):
    ms = jnp.mean(x * x, axis=-1, keepdims=True)
    return x * lax.rsqrt(ms + EPS) * g


LOG2E = 1.4426950408889634
F32_TINY = 1.1754943508222875e-38
GELU_C0 = -2.0 * 0.7978845608028654 * LOG2E
GELU_C1 = GELU_C0 * 0.044715


def _sigmoid(v):
    return 1.0 / (1.0 + jnp.exp2(v * (-LOG2E)))


def _gelu_tanh(v):
    return v / (1.0 + jnp.exp2(v * (GELU_C1 * (v * v) + GELU_C0)))


def _scan_pitch(tile):
    p = max(-(-tile // SUBLANES), POOL_HIST)
    while p % 8 != 4:
        p += 1
    return p


def _mlp_kernel(x_ref, g_ref, wu_ref, wd_ref, *rest, final_norm, ff_chunk):
    if final_norm:
        gf_ref, o_ref, a_scr = rest
    else:
        o_ref, a_scr = rest
    x = x_ref[0]
    h = _rms(x, g_ref[...]).astype(BF16)
    d_ff = wu_ref.shape[1]
    for c in range(d_ff // ff_chunk):
        cols = slice(c * ff_chunk, (c + 1) * ff_chunk)
        u = jnp.dot(h, wu_ref[:, cols], preferred_element_type=F32)
        r = jnp.maximum(u, 0.0)
        a_scr[:, cols] = (r * r).astype(BF16)
    y = x + jnp.dot(a_scr[...], wd_ref[...], preferred_element_type=F32)
    if final_norm:
        y = _rms(y, gf_ref[...])
    o_ref[0] = y


def _mlp(x, g, w_up, w_down, g_final, tile):
    B, L, D = x.shape
    d_ff = w_up.shape[1]
    final_norm = g_final is not None
    in_specs = [
        pl.BlockSpec((1, tile, D), lambda b, j: (b, j, 0)),
        _const_spec((1, D)),
        _const_spec((D, d_ff)),
        _const_spec((d_ff, D)),
    ]
    args = [x, g.reshape(1, D), w_up, w_down]
    if final_norm:
        in_specs.append(_const_spec((1, D)))
        args.append(g_final.reshape(1, D))
    return pl.pallas_call(
        functools.partial(_mlp_kernel, final_norm=final_norm, ff_chunk=4 * MXU_DIM),
        grid=(B, L // tile),
        in_specs=in_specs,
        out_specs=pl.BlockSpec((1, tile, D), lambda b, j: (b, j, 0)),
        out_shape=jax.ShapeDtypeStruct((B, L, D), F32),
        scratch_shapes=[pltpu.VMEM((tile, d_ff), BF16)],
        compiler_params=_cparams(),
        name="mlp",
    )(*args)


def _cross_kernel(x_ref, g_ref, wq_ref, wo_ref, *rest, packed):
    bt, tile, D = x_ref.shape
    dh = D // N_MEM_HEADS
    if packed:
        k_src, v_src, o_ref, att_scr = rest
    else:
        n_lb = dh // LANES
        mk_refs, mv_refs = rest[:n_lb], rest[n_lb:2 * n_lb]
        o_ref, att_scr, k_src, v_src = rest[2 * n_lb:]

        @pl.when(pl.program_id(1) == 0)
        def _():
            n_mem = k_src.shape[2]
            for src_refs, dst in ((mk_refs, k_src), (mv_refs, v_src)):
                for bi in range(bt):
                    for lb in range(n_lb):
                        rows = src_refs[lb].at[0, bi].reshape(n_mem * N_MEM_HEADS, LANES)
                        for hd in range(N_MEM_HEADS):
                            lanes = slice(hd * dh + lb * LANES, hd * dh + (lb + 1) * LANES)
                            dst[0, bi, :, lanes] = rows[pl.ds(hd, n_mem, stride=N_MEM_HEADS), :].astype(BF16)

    x = x_ref[...].reshape(bt * tile, D)
    h = _rms(x, g_ref[...]).astype(BF16)
    q = jnp.dot(h, wq_ref[...], preferred_element_type=F32)
    q = (q * (dh ** -0.5)).astype(BF16)
    def scores(bi, hd):
        return lax.dot_general(q[bi * tile:(bi + 1) * tile, hd * dh:(hd + 1) * dh],
                               k_src[0, bi, :, hd * dh:(hd + 1) * dh], (((1,), (1,)), ((), ())),
                               preferred_element_type=F32)

    def attend(bi, hd, s):
        mx = jnp.max(s, axis=-1, keepdims=True)
        p = jnp.exp(s - mx)
        den = jnp.sum(p, axis=-1, keepdims=True)
        o = jnp.dot(p.astype(BF16), v_src[0, bi, :, hd * dh:(hd + 1) * dh], preferred_element_type=F32)
        att_scr[bi * tile:(bi + 1) * tile, hd * dh:(hd + 1) * dh] = (o * (1.0 / den)).astype(BF16)

    units = [(bi, hd) for bi in range(bt) for hd in range(N_MEM_HEADS)]
    s_cur = scores(*units[0])
    for n, unit in enumerate(units):
        s_next = scores(*units[n + 1]) if n + 1 < len(units) else None
        attend(*unit, s_cur)
        s_cur = s_next
    y = x + jnp.dot(att_scr[...], wo_ref[...], preferred_element_type=F32)
    o_ref[...] = y.reshape(bt, tile, D)


def _cross(x, g, w_q, w_o, mem_k, mem_v, layer, tile, bt):
    B, L, D = x.shape
    n_mem = mem_k.shape[2]
    packed = mem_k.ndim == 4
    scratch = [pltpu.VMEM((bt * tile, D), BF16)]
    if packed:
        mem_specs = [pl.BlockSpec((1, bt, n_mem, D), lambda b, j: (layer, b, 0, 0))]
    else:
        n_heads, dh = mem_k.shape[3:]
        mem_specs = [pl.BlockSpec((1, bt, n_mem, n_heads, LANES),
                                  functools.partial(lambda b, j, lb: (layer, b, 0, 0, lb), lb=lb))
                     for lb in range(dh // LANES)]
        scratch += [pltpu.VMEM((1, bt, n_mem, D), BF16)] * 2
    n_blk = len(mem_specs)
    return pl.pallas_call(
        functools.partial(_cross_kernel, packed=packed),
        grid=(B // bt, L // tile),
        in_specs=[
            pl.BlockSpec((bt, tile, D), lambda b, j: (b, j, 0)),
            _const_spec((1, D)),
            _const_spec((D, D)),
            _const_spec((D, D)),
        ] + mem_specs + mem_specs,
        out_specs=pl.BlockSpec((bt, tile, D), lambda b, j: (b, j, 0)),
        out_shape=jax.ShapeDtypeStruct((B, L, D), F32),
        scratch_shapes=scratch,
        compiler_params=_cparams(),
        name="cross_attn",
    )(x, g.reshape(1, D), w_q, w_o, *([mem_k] * n_blk), *([mem_v] * n_blk))


def _memkv_kernel(m_ref, g_ref, wk_ref, wv_ref, k_ref, v_ref, kp_ref, vp_ref):
    bb, n_mem, D = m_ref.shape
    dh = k_ref.shape[4]
    mn = _rms(m_ref[...].reshape(bb * n_mem, D), g_ref[0]).astype(BF16)
    k = jnp.dot(mn, wk_ref[0], preferred_element_type=F32)
    v = jnp.dot(mn, wv_ref[0], preferred_element_type=F32)
    kp_ref[0] = k.astype(BF16).reshape(bb, n_mem, D)
    vp_ref[0] = v.astype(BF16).reshape(bb, n_mem, D)
    for bi in range(bb):
        rows = slice(bi * n_mem, (bi + 1) * n_mem)
        for hd in range(N_MEM_HEADS):
            k_ref[0, bi, :, hd, :] = k[rows, hd * dh:(hd + 1) * dh]
            v_ref[0, bi, :, hd, :] = v[rows, hd * dh:(hd + 1) * dh]


def _mem_kv(mem, g, w_k, w_v, bb):
    depth, D = g.shape
    B, n_mem, _ = mem.shape
    dh = D // N_MEM_HEADS
    out = jax.ShapeDtypeStruct((depth, B, n_mem, N_MEM_HEADS, dh), F32)
    out_spec = pl.BlockSpec((1, bb, n_mem, N_MEM_HEADS, dh), lambda l, j: (l, j, 0, 0, 0))
    packed = jax.ShapeDtypeStruct((depth, B, n_mem, D), BF16)
    packed_spec = pl.BlockSpec((1, bb, n_mem, D), lambda l, j: (l, j, 0, 0))
    return pl.pallas_call(
        _memkv_kernel,
        grid=(depth, B // bb),
        in_specs=[
            pl.BlockSpec((bb, n_mem, D), lambda l, j: (j, 0, 0)),
            pl.BlockSpec((1, 1, D), lambda l, j: (l, 0, 0)),
            pl.BlockSpec((1, D, D), lambda l, j: (l, 0, 0)),
            pl.BlockSpec((1, D, D), lambda l, j: (l, 0, 0)),
        ],
        out_specs=[out_spec, out_spec, packed_spec, packed_spec],
        out_shape=[out, out, packed, packed],
        compiler_params=_cparams(),
        name="mem_kv",
    )(mem, g.reshape(depth, 1, D), w_k, w_v)


def _even_kernel(x_ref, g_ref, win_ref, pbd_ref, pscale_ref, cw_ref, cb_ref, wax_ref, ba_ref, bx_ref,
                 lam_ref, wout_ref, pool0_ref, conv0_ref, lru0_ref,
                 o_ref, pool_o, conv_o, lru_o,
                 hcar, *sub_scratch, pos0, pitch):
    j = pl.program_id(1)
    n_sub = len(sub_scratch) // 2
    zs_list, ys_list = sub_scratch[:n_sub], sub_scratch[n_sub:]
    n_in = zs_list[0].shape[0]
    tl = x_ref.shape[1] // n_sub
    dp = pbd_ref.shape[0] * pbd_ref.shape[1]
    dr = cw_ref.shape[1]
    n_pool = dp // LANES
    n_rnn = dr // LANES
    rows_p = SUBLANES * pitch
    half = MXU_DIM
    per_half = half // LANES

    @pl.when(j == 0)
    def _():
        zs_tail = zs_list[n_sub - 1]
        for k in range(n_pool):
            zs_tail[k, tl - POOL_HIST:tl, :] = pool0_ref[0, :, k * LANES:(k + 1) * LANES]
        for k in range(n_rnn):
            zs_tail[n_pool + k, tl - CONV_HIST:tl, :] = conv0_ref[0, :, k * LANES:(k + 1) * LANES]
        hcar[...] = lru0_ref[0]
        for zs in zs_list:
            zs[:, tl:, :] = jnp.zeros((n_in, rows_p - tl, LANES), F32)

    subs = [_even_subtile(s, n_sub, tl, x_ref, g_ref, win_ref, pbd_ref, pscale_ref, cw_ref, cb_ref, wax_ref,
                          ba_ref, bx_ref, lam_ref, wout_ref, o_ref, pool_o, conv_o, lru_o,
                          zs_list[(s - 1) % n_sub], zs_list[s], ys_list[s], hcar,
                          pos0 + (j * n_sub + s) * tl, pitch) for s in range(n_sub)]
    next(subs[0])
    for s in range(n_sub):
        next(subs[s])
        if s + 1 < n_sub:
            next(subs[s + 1])
        for _ in subs[s]:
            pass


def _even_subtile(s, n_sub, tl, x_ref, g_ref, win_ref, pbd_ref, pscale_ref, cw_ref, cb_ref, wax_ref,
                  ba_ref, bx_ref, lam_ref, wout_ref, o_ref, pool_o, conv_o, lru_o,
                  zs_prev, zs, ys, hcar, pos_tile, pitch):
    dp = pbd_ref.shape[0] * pbd_ref.shape[1]
    dr = cw_ref.shape[1]
    n_pool = dp // LANES
    n_rnn = dr // LANES
    n_in = zs.shape[0]
    half = MXU_DIM
    per_half = half // LANES
    rows = slice(s * tl, (s + 1) * tl)
    is_last = s == n_sub - 1

    hist_pool = [zs_prev[k, tl - POOL_HIST:tl, :] for k in range(n_pool)]
    hist_x = [zs_prev[n_pool + k, tl - CONV_HIST:tl, :] for k in range(n_rnn)]

    x = x_ref[0, rows, :]
    h = _rms(x, g_ref[...]).astype(BF16)
    z = jnp.dot(h, win_ref[...], preferred_element_type=F32)
    for k in range(n_in):
        zs[k, 0:tl, :] = z[:, k * LANES:(k + 1) * LANES]
    if is_last:
        pool_o[0] = z[tl - POOL_HIST:, :dp]
        conv_o[0] = z[tl - CONV_HIST:, dp:dp + dr]
    yield

    def gather(k):
        return [zs.at[k][pl.ds(i, SUBLANES, stride=pitch), :] for i in range(pitch)]

    def stack(vregs):
        return jnp.concatenate(vregs, axis=0)

    def unstack(arr, lane_slab):
        return [arr[i * SUBLANES:(i + 1) * SUBLANES, lane_slab * LANES:(lane_slab + 1) * LANES]
                for i in range(pitch)]

    sub = lax.broadcasted_iota(jnp.int32, (SUBLANES, LANES), 0)
    first_chunk = sub == 0

    def with_history(cur, hist, depth):
        n_hist = hist.shape[0]
        wraps = []
        for m in range(depth, 0, -1):
            prev_end = pltpu.roll(cur[pitch - m], 1, axis=0)
            wraps.append(jnp.where(first_chunk, hist[n_hist - m:n_hist - m + 1, :], prev_end))
        return wraps + cur

    pos_first = pos_tile + sub * pitch
    d_slabs = []
    for gi, win in enumerate(POOL_WINDOWS):
        u = gather(gi)
        level = with_history(u, hist_pool[gi], win - 1)
        span = 1
        while span < win:
            level = [level[n] + level[n - span] for n in range(span, len(level))]
            span *= 2
        d = []
        for i in range(pitch):
            if i >= win - 1:
                inv_cnt = 1.0 / win
            else:
                inv_cnt = 1.0 / jnp.minimum(pos_first + (i + 1), win).astype(F32)
            d.append(level[i] * inv_cnt - u[i])
        d_slabs.append(stack(d))
    y_pool = []
    for hf in range(n_pool // per_half):
        cols = slice(hf * half, (hf + 1) * half)
        dd = jnp.concatenate(d_slabs[hf * per_half:(hf + 1) * per_half], axis=1).astype(BF16)
        yp = jnp.dot(dd, pbd_ref[hf], preferred_element_type=F32)
        y_pool.append((yp * pscale_ref[:, cols]).astype(BF16))

    xc_slabs = []
    for k in range(n_rnn):
        lanes = slice(k * LANES, (k + 1) * LANES)
        ext = with_history(gather(n_pool + k), hist_x[k], CONV_WIDTH - 1)
        taps = [jnp.broadcast_to(cw_ref[t:t + 1, lanes], (SUBLANES, LANES)) for t in range(CONV_WIDTH)]
        bias = jnp.broadcast_to(cb_ref[:, lanes], (SUBLANES, LANES))
        xc = []
        for i in range(pitch):
            acc = bias
            for t in range(CONV_WIDTH):
                acc = acc + ext[i + t] * taps[t]
            xc.append(acc)
        xc_slabs.append(stack(xc))

    c_lam = -RG_C * jnp.log1p(jnp.exp(-lam_ref[...]))
    t_last = tl - 1
    h_slabs, h_last = [], []
    xc_halves = [jnp.concatenate(xc_slabs[hf * per_half:(hf + 1) * per_half], axis=1)
                 for hf in range(n_rnn // per_half)]
    gate_halves = [jnp.dot(xc.astype(BF16), wax_ref[hf], preferred_element_type=F32)
                   for hf, xc in enumerate(xc_halves)]
    yield
    for hf in range(n_rnn // per_half):
        cols = slice(hf * half, (hf + 1) * half)
        xc, gates = xc_halves[hf], gate_halves[hf]
        r_gate = _sigmoid(gates[:, :half] + ba_ref[:, cols])
        i_gate = _sigmoid(gates[:, half:] + bx_ref[:, cols])
        a = jnp.exp2(r_gate * (c_lam[:, cols] * LOG2E))
        one_minus_a2 = jnp.tanh(r_gate * (-c_lam[:, cols])) * (a * a + 1.0)
        mult = one_minus_a2 * lax.rsqrt(jnp.maximum(one_minus_a2, F32_TINY))
        bb = mult * i_gate * xc
        for sl in range(per_half):
            k = hf * per_half + sl
            lanes = slice(k * LANES, (k + 1) * LANES)
            a_v, b_v = unstack(a, sl), unstack(bb, sl)
            hv = jnp.zeros((SUBLANES, LANES), F32)
            pv = jnp.ones((SUBLANES, LANES), F32)
            h_loc, p_loc = [], []
            for i in range(pitch):
                hv = a_v[i] * hv + b_v[i]
                pv = a_v[i] * pv
                h_loc.append(hv)
                p_loc.append(pv)
            carry = [hcar[:, lanes]]
            for r in range(1, SUBLANES):
                carry.append(hv[r - 1:r, :] + pv[r - 1:r, :] * carry[-1])
            cv = jnp.concatenate(carry, axis=0)
            h_fix = [h_loc[i] + p_loc[i] * cv for i in range(pitch)]
            last = h_fix[t_last % pitch][t_last // pitch:t_last // pitch + 1, :]
            hcar[:, lanes] = last
            h_last.append(last)
            h_slabs.append(stack(h_fix))
    if is_last:
        lru_o[0] = jnp.concatenate(h_last, axis=1)

    gate = jnp.concatenate([stack(gather(n_pool + n_rnn + k)) for k in range(n_rnn)], axis=1)
    y_rnn = (jnp.concatenate(h_slabs, axis=1) * _gelu_tanh(gate)).astype(BF16)
    cat = jnp.concatenate(y_pool + [y_rnn], axis=1)
    y = jnp.dot(cat, wout_ref[...], preferred_element_type=F32)
    for k in range(ys.shape[0]):
        for i in range(pitch):
            ys.at[k][pl.ds(i, SUBLANES, stride=pitch), :] = (
                y[i * SUBLANES:(i + 1) * SUBLANES, k * LANES:(k + 1) * LANES])
    o_ref[0, rows, :] = x + jnp.concatenate([ys[k, 0:tl, :] for k in range(ys.shape[0])], axis=1)


def _even_layer(x, g, wts, pool0, conv0, lru0, pos0, tile, n_sub):
    B, L, D = x.shape
    dp = pool0.shape[2]
    dr = conv0.shape[2]
    pitch = _scan_pitch(tile // n_sub)
    row = lambda v: v.reshape(1, -1)
    consts = [row(g), wts["w_in"], wts["pool_bd"], row(wts["pool_scale"]), wts["conv_w"],
              row(wts["conv_b"]), wts["w_ax"], row(wts["b_a"]), row(wts["b_x"]),
              row(wts["lam"]), wts["w_out"]]
    in_specs = [pl.BlockSpec((1, tile, D), lambda b, j: (b, j, 0))]
    in_specs += [_const_spec(c.shape) for c in consts]
    in_specs += [
        pl.BlockSpec((1, POOL_HIST, dp), lambda b, j: (b, 0, 0)),
        pl.BlockSpec((1, CONV_HIST, dr), lambda b, j: (b, 0, 0)),
        pl.BlockSpec((1, 1, dr), lambda b, j: (b, 0, 0)),
    ]
    d_in = wts["w_in"].shape[1]
    return pl.pallas_call(
        functools.partial(_even_kernel, pos0=pos0, pitch=pitch),
        grid=(B, L // tile),
        in_specs=in_specs,
        out_specs=[
            pl.BlockSpec((1, tile, D), lambda b, j: (b, j, 0)),
            pl.BlockSpec((1, POOL_HIST, dp), lambda b, j: (b, 0, 0)),
            pl.BlockSpec((1, CONV_HIST, dr), lambda b, j: (b, 0, 0)),
            pl.BlockSpec((1, 1, dr), lambda b, j: (b, 0, 0)),
        ],
        out_shape=[
            jax.ShapeDtypeStruct((B, L, D), F32),
            jax.ShapeDtypeStruct((B, POOL_HIST, dp), F32),
            jax.ShapeDtypeStruct((B, CONV_HIST, dr), F32),
            jax.ShapeDtypeStruct((B, 1, dr), F32),
        ],
        scratch_shapes=(
            [pltpu.VMEM((1, dr), F32)]
            + [pltpu.VMEM((d_in // LANES, SUBLANES * pitch, LANES), F32)] * n_sub
            + [pltpu.VMEM((D // LANES, SUBLANES * pitch, LANES), F32)] * n_sub),
        compiler_params=_cparams(),
        name="even_mixer",
    )(x, *consts, pool0, conv0, lru0)


def _odd_kernel(x_ref, g_ref, wqkv_ref, wo_ref, sink_ref, ones_ref, kprev_ref, vprev_ref,
                o_ref, ktail_ref, vtail_ref,
                q_scr, k_buf, v_buf, att_scr, *, prefix_valid):
    j = pl.program_id(1)
    bt, tq, D = x_ref.shape
    nq = q_scr.shape[1]
    nkv = k_buf.shape[2]
    group = nq // nkv
    n_chunks = tq // CHUNK

    @pl.when(j == 0)
    def _():
        pad = jnp.zeros((bt, KV_HIST - WINDOW, nkv), BF16)
        k_buf[:, 0:KV_HIST - WINDOW, :] = pad
        v_buf[:, 0:KV_HIST - WINDOW, :] = pad
        k_buf[:, KV_HIST - WINDOW:KV_HIST, :] = kprev_ref[...].astype(BF16)
        v_buf[:, KV_HIST - WINDOW:KV_HIST, :] = vprev_ref[...].astype(BF16)

    @pl.when(j > 0)
    def _():
        k_buf[:, 0:KV_HIST, :] = k_buf[:, tq:tq + KV_HIST, :]
        v_buf[:, 0:KV_HIST, :] = v_buf[:, tq:tq + KV_HIST, :]

    x = x_ref[...].reshape(bt * tq, D)
    h = _rms(x, g_ref[...]).astype(BF16)
    z = jnp.dot(h, wqkv_ref[...], preferred_element_type=F32)
    q_scr[...] = (z[:, :nq] * (HEAD_DIM ** -0.5)).astype(BF16)
    for bi in range(bt):
        k_new = z[bi * tq:(bi + 1) * tq, nq:nq + nkv]
        v_new = z[bi * tq:(bi + 1) * tq, nq + nkv:]
        k_buf[bi, KV_HIST:KV_HIST + tq, :] = k_new.astype(BF16)
        v_buf[bi, KV_HIST:KV_HIST + tq, :] = v_new.astype(BF16)
        if tq >= WINDOW:
            ktail_ref[bi] = k_new[tq - WINDOW:, :]
            vtail_ref[bi] = v_new[tq - WINDOW:, :]
        else:
            ktail_ref[bi] = jnp.concatenate([kprev_ref[bi][tq:, :], k_new], axis=0)
            vtail_ref[bi] = jnp.concatenate([vprev_ref[bi][tq:, :], v_new], axis=0)

    lane = lax.broadcasted_iota(jnp.int32, (1, nkv), 1)
    head_masks = [(lane // HEAD_DIM) == kv for kv in range(N_KV_HEADS)]
    key_idx = lax.broadcasted_iota(jnp.int32, (1, KEY_SPAN), 1)
    pair_w = 2 * HEAD_DIM
    first_head = lax.broadcasted_iota(jnp.int32, (1, pair_w), 1) < HEAD_DIM

    def chunk_scores(bi, c):
        r0 = bi * tq + c * CHUNK
        qs = jnp.concatenate([q_scr[r0:r0 + CHUNK, gi * nkv:(gi + 1) * nkv] for gi in range(group)],
                             axis=0)
        kspan = k_buf[bi, c * CHUNK:c * CHUNK + KEY_SPAN, :]
        zero = jnp.zeros_like(kspan)
        kbd = jnp.concatenate([jnp.where(m, kspan, zero) for m in head_masks], axis=0)
        return lax.dot_general(qs, kbd, (((1,), (1,)), ((), ())), preferred_element_type=F32)

    def chunk_attend(bi, c, s_all):
        r0 = bi * tq + c * CHUNK
        vspan = v_buf[bi, c * CHUNK:c * CHUNK + KEY_SPAN, :]
        key_pos = (j * tq - KV_HIST) + c * CHUNK + key_idx
        valid = key_idx >= (KEY_SPAN - WINDOW - CHUNK)
        if not prefix_valid:
            valid = valid & (key_pos >= 0)
        ps, mxs = [], []
        for kv in range(N_KV_HEADS):
            s = jnp.where(valid, s_all[:, kv * KEY_SPAN:(kv + 1) * KEY_SPAN], NEG_INF)
            mx = jnp.max(s, axis=-1, keepdims=True)
            ps.append(jnp.exp(s - mx).astype(BF16))
            mxs.append(mx)
        for pr in range(N_KV_HEADS // 2):
            lanes = slice(pr * pair_w, (pr + 1) * pair_w)
            vs = vspan[:, lanes]
            zero_v = jnp.zeros_like(vs)
            rhs = jnp.concatenate(
                [jnp.concatenate([jnp.where(first_head, vs, zero_v), jnp.where(first_head, zero_v, vs)],
                                 axis=0), ones_ref[...]], axis=1)
            o_all = jnp.dot(jnp.concatenate(ps[2 * pr:2 * pr + 2], axis=1), rhs,
                            preferred_element_type=F32)
            mx_pair = jnp.where(first_head, mxs[2 * pr], mxs[2 * pr + 1])
            den = o_all[:, pair_w:] + jnp.exp(sink_ref[:, lanes] - mx_pair)
            o = (o_all[:, :pair_w] * (1.0 / den)).astype(BF16)
            for gi in range(group):
                att_scr[r0:r0 + CHUNK, gi * nkv + pr * pair_w:gi * nkv + (pr + 1) * pair_w] = (
                    o[gi * CHUNK:(gi + 1) * CHUNK, :])

    units = [(bi, c) for bi in range(bt) for c in range(n_chunks)]
    s_cur = chunk_scores(*units[0])
    for n, unit in enumerate(units):
        s_next = chunk_scores(*units[n + 1]) if n + 1 < len(units) else None
        chunk_attend(*unit, s_cur)
        s_cur = s_next

    y = x + jnp.dot(att_scr[...], wo_ref[...], preferred_element_type=F32)
    o_ref[...] = y.reshape(bt, tq, D)


def _odd_layer(x, g, w_qkv, w_o, sink_tab, k_prev, v_prev, prefix_valid, tile, bt):
    B, L, D = x.shape
    nkv = k_prev.shape[2]
    nq = w_o.shape[0]
    head_ones = (jnp.arange(2 * KEY_SPAN)[:, None] // KEY_SPAN
                 == jnp.arange(2 * HEAD_DIM)[None, :] // HEAD_DIM).astype(BF16)
    tail = jax.ShapeDtypeStruct((B, WINDOW, nkv), F32)
    return pl.pallas_call(
        functools.partial(_odd_kernel, prefix_valid=prefix_valid),
        grid=(B // bt, L // tile),
        in_specs=[
            pl.BlockSpec((bt, tile, D), lambda b, j: (b, j, 0)),
            _const_spec((1, D)),
            _const_spec(w_qkv.shape),
            _const_spec(w_o.shape),
            _const_spec(sink_tab.shape),
            _const_spec(head_ones.shape),
            pl.BlockSpec((bt, WINDOW, nkv), lambda b, j: (b, 0, 0)),
            pl.BlockSpec((bt, WINDOW, nkv), lambda b, j: (b, 0, 0)),
        ],
        out_specs=[
            pl.BlockSpec((bt, tile, D), lambda b, j: (b, j, 0)),
            pl.BlockSpec((bt, WINDOW, nkv), lambda b, j: (b, 0, 0)),
            pl.BlockSpec((bt, WINDOW, nkv), lambda b, j: (b, 0, 0)),
        ],
        out_shape=[jax.ShapeDtypeStruct((B, L, D), F32), tail, tail],
        scratch_shapes=[
            pltpu.VMEM((bt * tile, nq), BF16),
            pltpu.VMEM((bt, KV_HIST + tile, nkv), BF16),
            pltpu.VMEM((bt, KV_HIST + tile, nkv), BF16),
            pltpu.VMEM((bt * tile, nq), BF16),
        ],
        compiler_params=_cparams(),
        name="swa_mixer",
    )(x, g.reshape(1, D), w_qkv, w_o, sink_tab, head_ones, k_prev, v_prev)


def _block_diag(blocks):
    n, r, c = blocks.shape
    eye = jnp.eye(n, dtype=blocks.dtype)
    return (eye[:, None, :, None] * blocks[:, :, None, :]).reshape(n * r, n * c)


def _prep_even(p, e):
    pool_w = p["pool_w"][e]
    per_tile = MXU_DIM // pool_w.shape[1]
    pool_bd = jnp.stack([_block_diag(pool_w[i:i + per_tile])
                         for i in range(0, pool_w.shape[0], per_tile)])
    wa, wx = p["w_rg_a"][e], p["w_rg_x"][e]
    per_tile = MXU_DIM // wa.shape[1]
    w_ax = jnp.stack([jnp.concatenate([_block_diag(wa[i:i + per_tile]), _block_diag(wx[i:i + per_tile])],
                                      axis=1)
                      for i in range(0, wa.shape[0], per_tile)])
    return dict(
        w_in=p["w_in_even"][e].astype(BF16), pool_bd=pool_bd.astype(BF16), pool_scale=p["pool_scale"][e],
        conv_w=p["conv_w"][e], conv_b=p["conv_b"][e], w_ax=w_ax.astype(BF16), b_a=p["b_rg_a"][e],
        b_x=p["b_rg_x"][e], lam=p["rg_lambda"][e], w_out=p["w_out_even"][e].astype(BF16))


def _prep_odd(p, o):
    w_qkv = p["w_qkv_odd"][o]
    D = w_qkv.shape[0]
    w_o = p["w_o_odd"][o]
    nq = w_o.shape[0]
    group = nq // (N_KV_HEADS * HEAD_DIM)
    wq = w_qkv[:, :nq].reshape(D, N_KV_HEADS, group, HEAD_DIM).transpose(0, 2, 1, 3).reshape(D, nq)
    w_qkv = jnp.concatenate([wq, w_qkv[:, nq:]], axis=1)
    w_o = w_o.reshape(N_KV_HEADS, group, HEAD_DIM, -1).transpose(1, 0, 2, 3).reshape(nq, -1)
    sinks = p["attn_sinks"][o].reshape(N_KV_HEADS, group).T
    sink_tab = jnp.repeat(jnp.repeat(sinks, CHUNK, axis=0), HEAD_DIM, axis=1).astype(F32)
    return dict(w_qkv=w_qkv.astype(BF16), w_o=w_o.astype(BF16), sink_tab=sink_tab)


def _pad_rows(a, rows):
    return jnp.pad(a, ((0, 0), (rows - a.shape[1], 0), (0, 0)))


def _trunk(x, pos0, prefix_valid, pool_st, conv_st, lru_st, swa_k, swa_v, mem_k, mem_v, p, prep):
    depth = p["norm_mix"].shape[0]
    B, L = x.shape[:2]
    even_sub = min(L, EVEN_SUBTILE)
    even_n_sub = min(L // even_sub, EVEN_SUBTILES)
    new_pool, new_conv, new_lru, new_k, new_v = [], [], [], [], []
    for layer in range(depth):
        if layer % 2 == 0:
            e = layer // 2
            x, pn, cn, ln = _even_layer(
                x, p["norm_mix"][layer], prep["even"][e],
                _pad_rows(pool_st[e], POOL_HIST), _pad_rows(conv_st[e], CONV_HIST),
                lru_st[e][:, None, :], pos0, even_sub * even_n_sub, even_n_sub)
            new_pool.append(pn[:, POOL_HIST - pool_st.shape[2]:])
            new_conv.append(cn[:, CONV_HIST - conv_st.shape[2]:])
            new_lru.append(ln[:, 0])
        else:
            o = layer // 2
            kp = swa_k[o].reshape(B, WINDOW, -1)
            vp = swa_v[o].reshape(B, WINDOW, -1)
            x, kn, vn = _odd_layer(x, p["norm_mix"][layer], prep["odd"][o]["w_qkv"], prep["odd"][o]["w_o"],
                                   prep["odd"][o]["sink_tab"], kp, vp, prefix_valid, min(L, SWA_TILE),
                                   min(max(1, SWA_TILE // L), B))
            new_k.append(kn.reshape(swa_k[o].shape))
            new_v.append(vn.reshape(swa_v[o].shape))
        bt = max(1, CROSS_TILE // L)
        x = _cross(x, p["norm_cross"][layer], prep["w_mq"][layer], prep["w_mo"][layer],
                   mem_k, mem_v, layer, min(CROSS_TILE, L), min(bt, CROSS_MAX_BATCH_BLOCK, B))
        g_final = p["norm_final"] if layer == depth - 1 else None
        mlp_w = (p["norm_mlp"][layer], prep["w_up"][layer], prep["w_down"][layer], g_final)
        if L >= MLP_TILE:
            x = _mlp(x, *mlp_w, MLP_TILE)
        else:
            x = _mlp(x.reshape(1, B * L, -1), *mlp_w, min(MLP_TILE, B * L)).reshape(B, L, -1)
    return (x, jnp.stack(new_pool), jnp.stack(new_conv), jnp.stack(new_lru),
            jnp.stack(new_k), jnp.stack(new_v))


def kernel(x_prompt, x_sample, state_pool, state_conv, state_lru, cache_swa_k, cache_swa_v,
           cache_mem_k, cache_mem_v, mem_prompt, norm_mix, norm_cross, norm_mem, norm_mlp, norm_final,
           w_in_even, conv_w, conv_b, w_rg_a, b_rg_a, w_rg_x, b_rg_x, rg_lambda, pool_w, pool_scale,
           w_out_even, w_qkv_odd, attn_sinks, w_o_odd, w_mq, w_mk, w_mv, w_mo, w_up, w_down):
    p = dict(norm_mix=norm_mix, norm_cross=norm_cross, norm_mlp=norm_mlp, norm_final=norm_final,
             w_in_even=w_in_even, conv_w=conv_w, conv_b=conv_b, w_rg_a=w_rg_a, b_rg_a=b_rg_a,
             w_rg_x=w_rg_x, b_rg_x=b_rg_x, rg_lambda=rg_lambda, pool_w=pool_w, pool_scale=pool_scale,
             w_out_even=w_out_even, w_qkv_odd=w_qkv_odd, attn_sinks=attn_sinks, w_o_odd=w_o_odd)
    depth = norm_mix.shape[0]
    n_even, n_odd = state_pool.shape[0], cache_swa_k.shape[0]
    prep = dict(
        even=[_prep_even(p, e) for e in range(n_even)],
        odd=[_prep_odd(p, o) for o in range(n_odd)],
        w_mq=w_mq.astype(BF16), w_mo=w_mo.astype(BF16),
        w_up=w_up.astype(BF16), w_down=w_down.astype(BF16))

    B, L, D = x_prompt.shape
    dt = x_prompt.dtype
    mem_k_p, mem_v_p, mem_k_packed, mem_v_packed = _mem_kv(
        mem_prompt, norm_mem, w_mk.astype(BF16), w_mv.astype(BF16), bb=2)

    zero_pool = jnp.zeros((n_even, B) + state_pool.shape[2:], dt)
    zero_conv = jnp.zeros((n_even, B) + state_conv.shape[2:], dt)
    zero_lru = jnp.zeros((n_even, B) + state_lru.shape[2:], dt)
    zero_kv = jnp.zeros((n_odd, B) + cache_swa_k.shape[2:], dt)
    y_prompt, pool_p, conv_p, lru_p, swa_k_p, swa_v_p = _trunk(
        x_prompt, 0, False, zero_pool, zero_conv, zero_lru, zero_kv, zero_kv, mem_k_packed, mem_v_packed,
        p, prep)
    y_sample, pool_s, conv_s, lru_s, swa_k_s, swa_v_s = _trunk(
        x_sample, PAST_LEN, True, state_pool, state_conv, state_lru, cache_swa_k, cache_swa_v,
        cache_mem_k, cache_mem_v, p, prep)
    return (y_prompt, y_sample, pool_p, conv_p, lru_p, swa_k_p, swa_v_p, mem_k_p, mem_v_p,
            pool_s, conv_s, lru_s, swa_k_s, swa_v_s)
```
